```python
import jax
import jax.numpy as jnp
from jax import lax
import numpy as np

D_MODEL = 1024
BATCH = 1
SEQ = 16384
DEPTH = 4
DEC_BATCH = 8
DEC_SEQ = 32
PAST_LEN = 4096

CHUNK = 64
EPS = 1e-6
A_HEADS = 4
A_DK = 128
A_DV = 128
A_QK = A_HEADS * A_DK
A_VW = A_HEADS * A_DV
A_QKV = 2 * A_QK + A_VW
CONV_W = 4
B_HEADS = 8
B_KV_HEADS = 2
B_GROUP = B_HEADS // B_KV_HEADS
B_HD = 64
B_QW = B_HEADS * B_HD
B_KW = B_KV_HEADS * B_HD
WINDOW = 128
WIN_CHUNKS = WINDOW // CHUNK
ROPE_DIM = B_HD // 4
ROPE_THETA = 500000.0
D_MIX = A_VW + B_QW
D_IN = A_QKV + A_VW + 2 * A_HEADS + B_QW + 2 * B_KW
D_FF = 2816

kernel_name = "hybrid_stream_gdn_swa_step"


def rmsnorm(x, w):
    xf = x.astype(jnp.float32)
    y = xf * lax.rsqrt(jnp.mean(xf * xf, axis=-1, keepdims=True) + EPS)
    return (y * w.astype(jnp.float32)).astype(x.dtype)


def swiglu(h, w_in, w_out):
    gate, up = jnp.split(h @ w_in, 2, axis=-1)
    return (jax.nn.silu(gate) * up) @ w_out


def l2norm(x):
    return x * lax.rsqrt(jnp.sum(x * x, axis=-1, keepdims=True) + EPS)


def partial_rope(x, pos):
    half = ROPE_DIM // 2
    inv = jnp.power(ROPE_THETA, -jnp.arange(0, ROPE_DIM, 2, dtype=jnp.float32) / ROPE_DIM)
    ang = pos.astype(jnp.float32)[:, None] * inv[None, :]
    cos = jnp.cos(ang)[None, :, None, :]
    sin = jnp.sin(ang)[None, :, None, :]
    xf = x.astype(jnp.float32)
    x1, x2, rest = xf[..., :half], xf[..., half:ROPE_DIM], xf[..., ROPE_DIM:]
    out = jnp.concatenate([x1 * cos - x2 * sin, x2 * cos + x1 * sin, rest], axis=-1)
    return out.astype(x.dtype)


def chunked_gated_delta(q, k, v, g, beta, s0, chunk):
    B, L, H, DK = q.shape
    DV = v.shape[-1]
    n = L // chunk

    def blk(t):
        return t.reshape(B, n, chunk, H, t.shape[-1]).transpose(1, 0, 3, 2, 4)

    def blk_s(t):
        return t.reshape(B, n, chunk, H).transpose(1, 0, 3, 2)

    qc, kc, vc = blk(q), blk(k), blk(v)
    bc = blk_s(beta)
    G = jnp.cumsum(blk_s(g), axis=-1)
    idx = jnp.arange(chunk)
    causal = idx[:, None] >= idx[None, :]
    strict = idx[:, None] > idx[None, :]
    decay = jnp.exp(jnp.where(causal, G[..., :, None] - G[..., None, :], -jnp.inf))
    kk = jnp.einsum('nbhid,nbhjd->nbhij', kc, kc)
    lmat = jnp.where(strict, kk * decay * bc[..., :, None], 0.0)
    amat = lmat + jnp.eye(chunk, dtype=lmat.dtype)
    rhs = jnp.concatenate([vc * bc[..., None], kc * (bc * jnp.exp(G))[..., None]], axis=-1)
    sol = lax.linalg.triangular_solve(amat, rhs, left_side=True, lower=True, unit_diagonal=True)
    u, w = sol[..., :DV], sol[..., DV:]
    qk = jnp.einsum('nbhid,nbhjd->nbhij', qc, kc) * decay
    q_dec = qc * jnp.exp(G)[..., None]
    k_dec = kc * jnp.exp(G[..., -1:] - G)[..., None]
    g_tot = jnp.exp(G[..., -1])

    def step(S, xs):
        u_i, w_i, q_i, k_i, qk_i, gt_i = xs
        v_new = u_i - jnp.einsum('bhck,bhkv->bhcv', w_i, S)
        o = jnp.einsum('bhck,bhkv->bhcv', q_i, S) + jnp.einsum('bhij,bhjv->bhiv', qk_i, v_new)
        S = S * gt_i[..., None, None] + jnp.einsum('bhck,bhcv->bhkv', k_i, v_new)
        return S, o

    S, o = lax.scan(step, s0, (u, w, q_dec, k_dec, qk, g_tot))
    return o.transpose(1, 0, 3, 2, 4).reshape(B, L, H, DV), S


def gated_deltanet(qkv, z, b_logit, a_logit, conv_buf, s0, conv_w, a_log, dt_bias, gnorm_w, chunk):
    B, L, _ = qkv.shape
    xp = jnp.concatenate([conv_buf.astype(qkv.dtype), qkv], axis=1)
    conv = sum(xp[:, j:j + L] * conv_w[j] for j in range(CONV_W))
    new_buf = xp[:, L:]
    act = jax.nn.silu(conv.astype(jnp.float32))
    q = l2norm(act[..., :A_QK].reshape(B, L, A_HEADS, A_DK)) * (A_DK ** -0.5)
    k = l2norm(act[..., A_QK:2 * A_QK].reshape(B, L, A_HEADS, A_DK))
    v = act[..., 2 * A_QK:].reshape(B, L, A_HEADS, A_DV)
    beta = jax.nn.sigmoid(b_logit.astype(jnp.float32))
    g = -jnp.exp(a_log.astype(jnp.float32)) * jax.nn.softplus(a_logit.astype(jnp.float32) + dt_bias.astype(jnp.float32))
    o, s_new = chunked_gated_delta(q, k, v, g, beta, s0.astype(jnp.float32), chunk)
    o = o * lax.rsqrt(jnp.mean(o * o, axis=-1, keepdims=True) + EPS) * gnorm_w.astype(jnp.float32)
    o = o * jax.nn.silu(z.astype(jnp.float32).reshape(B, L, A_HEADS, A_DV))
    return o.reshape(B, L, A_VW).astype(qkv.dtype), new_buf, s_new.astype(s0.dtype)


def sink_attention(q, k, v, valid, sinks):
    s = jnp.einsum('bnqhgd,bnkhd->bnhgqk', q, k).astype(jnp.float32) * (B_HD ** -0.5)
    if valid is not None:
        s = jnp.where(valid, s, -jnp.inf)
    sk = sinks.astype(jnp.float32).reshape(B_KV_HEADS, B_GROUP)[None, None, :, :, None, None]
    m = jnp.maximum(jnp.max(s, axis=-1, keepdims=True), sk)
    p = jnp.exp(s - m)
    denom = jnp.sum(p, axis=-1, keepdims=True) + jnp.exp(sk - m)
    return jnp.einsum('bnhgqk,bnkhd->bnqhgd', (p / denom).astype(v.dtype), v)


def swa_prompt(q, k, v, sinks):
    B, L = q.shape[:2]
    n = L // CHUNK
    pad = WIN_CHUNKS * CHUNK

    def band(t):
        tp = jnp.pad(t, ((0, 0), (pad, 0), (0, 0), (0, 0))).reshape(B, n + WIN_CHUNKS, CHUNK, B_KV_HEADS, B_HD)
        return jnp.concatenate([tp[:, j:j + n] for j in range(WIN_CHUNKS + 1)], axis=2)

    kpos = (jnp.arange(n)[:, None] - WIN_CHUNKS) * CHUNK + jnp.arange((WIN_CHUNKS + 1) * CHUNK)[None, :]
    valid = (kpos >= 0)[None, :, None, None, None, :]
    o = sink_attention(q.reshape(B, n, CHUNK, B_KV_HEADS, B_GROUP, B_HD), band(k), band(v), valid, sinks)
    return o.reshape(B, L, B_QW)


def swa_sample(q, k, v, k_cache, v_cache, sinks):
    B, L = q.shape[:2]
    k_all = jnp.concatenate([k_cache.astype(k.dtype), k], axis=1)
    v_all = jnp.concatenate([v_cache.astype(v.dtype), v], axis=1)
    o = sink_attention(q.reshape(B, 1, L, B_KV_HEADS, B_GROUP, B_HD), k_all[:, None], v_all[:, None], None, sinks)
    rows = k_cache.shape[1]
    return o.reshape(B, L, B_QW), k_all[:, -rows:], v_all[:, -rows:]


def token_mix(h, pos, chunk, conv_buf, s0, k_cache, v_cache, w_in, conv_w, a_log, dt_bias, gnorm_w, sinks, w_out):
    B, L, _ = h.shape
    proj = h @ w_in
    offs = np.cumsum([A_QKV, A_VW, A_HEADS, A_HEADS, B_QW, B_KW]).tolist()
    qkv_a, z_a, b_a, a_a, q_b, k_b, v_b = jnp.split(proj, offs, axis=-1)
    o_a, new_buf, s_new = gated_deltanet(qkv_a, z_a, b_a, a_a, conv_buf, s0, conv_w, a_log, dt_bias, gnorm_w, chunk)
    q_b = partial_rope(q_b.reshape(B, L, B_HEADS, B_HD), pos)
    k_b = partial_rope(k_b.reshape(B, L, B_KV_HEADS, B_HD), pos)
    v_b = v_b.reshape(B, L, B_KV_HEADS, B_HD)
    if k_cache is None:
        o_b = swa_prompt(q_b, k_b, v_b, sinks)
        rows = min(WINDOW, PAST_LEN)
        new_k, new_v = k_b[:, L - rows:], v_b[:, L - rows:]
    else:
        o_b, new_k, new_v = swa_sample(q_b, k_b, v_b, k_cache, v_cache, sinks)
    out = jnp.concatenate([o_a, o_b], axis=-1) @ w_out
    return out, new_buf, s_new, new_k, new_v


def trunk(x, pos, chunk, conv_bufs, s0s, k_caches, v_caches,
          norm_ff1, ff1_w_in, ff1_w_out, norm_mix, w_mix_in, conv_w, a_log, dt_bias,
          gnorm_w, sinks, w_mix_out, norm_ff2, ff2_w_in, ff2_w_out, norm_final):
    bufs, states, ks, vs = [], [], [], []
    for l in range(DEPTH):
        x = x + 0.5 * swiglu(rmsnorm(x, norm_ff1[l]), ff1_w_in[l], ff1_w_out[l])
        kc = None if k_caches is None else k_caches[l]
        vc = None if v_caches is None else v_caches[l]
        o, b, s, kn, vn = token_mix(rmsnorm(x, norm_mix[l]), pos, chunk, conv_bufs[l], s0s[l], kc, vc,
                                    w_mix_in[l], conv_w[l], a_log[l], dt_bias[l], gnorm_w[l], sinks[l], w_mix_out[l])
        x = x + o
        x = x + 0.5 * swiglu(rmsnorm(x, norm_ff2[l]), ff2_w_in[l], ff2_w_out[l])
        bufs.append(b)
        states.append(s)
        ks.append(kn)
        vs.append(vn)
    return rmsnorm(x, norm_final), jnp.stack(bufs), jnp.stack(states), jnp.stack(ks), jnp.stack(vs)


def setup_inputs(seed: int = 0) -> dict:
    key = jax.random.key(seed)
    ks = jax.random.split(key, 24)
    f32 = jnp.float32

    def nrm(k, shape, scale):
        return jax.random.normal(k, shape, f32) * scale

    rows = min(WINDOW, PAST_LEN)
    dt = jnp.exp(jax.random.uniform(ks[13], (DEPTH, A_HEADS), f32, float(np.log(1e-3)), float(np.log(1e-1))))
    return {
        "x_prompt": nrm(ks[0], (BATCH, SEQ, D_MODEL), 1.0),
        "x_sample": nrm(ks[1], (DEC_BATCH, DEC_SEQ, D_MODEL), 1.0),
        "cache_conv": nrm(ks[2], (DEPTH, DEC_BATCH, CONV_W - 1, A_QKV), 1.0),
        "state_delta": nrm(ks[3], (DEPTH, DEC_BATCH, A_HEADS, A_DK, A_DV), 0.1),
        "cache_k": nrm(ks[4], (DEPTH, DEC_BATCH, rows, B_KV_HEADS, B_HD), 1.0),
        "cache_v": nrm(ks[5], (DEPTH, DEC_BATCH, rows, B_KV_HEADS, B_HD), 1.0),
        "norm_ff1": 1.0 + nrm(ks[6], (DEPTH, D_MODEL), 0.02),
        "ff1_w_in": nrm(ks[7], (DEPTH, D_MODEL, 2 * D_FF), D_MODEL ** -0.5),
        "ff1_w_out": nrm(ks[8], (DEPTH, D_FF, D_MODEL), D_FF ** -0.5),
        "norm_mix": 1.0 + nrm(ks[9], (DEPTH, D_MODEL), 0.02),
        "w_mix_in": nrm(ks[10], (DEPTH, D_MODEL, D_IN), D_MODEL ** -0.5),
        "conv_w": nrm(ks[11], (DEPTH, CONV_W, A_QKV), 0.5),
        "a_log": jnp.log(jax.random.uniform(ks[12], (DEPTH, A_HEADS), f32, 1.0, 16.0)),
        "dt_bias": dt + jnp.log(-jnp.expm1(-dt)),
        "gnorm_w": 1.0 + nrm(ks[14], (DEPTH, A_DV), 0.02),
        "sinks": nrm(ks[15], (DEPTH, B_HEADS), 1.0),
        "w_mix_out": nrm(ks[16], (DEPTH, D_MIX, D_MODEL), D_MIX ** -0.5),
        "norm_ff2": 1.0 + nrm(ks[17], (DEPTH, D_MODEL), 0.02),
        "ff2_w_in": nrm(ks[18], (DEPTH, D_MODEL, 2 * D_FF), D_MODEL ** -0.5),
        "ff2_w_out": nrm(ks[19], (DEPTH, D_FF, D_MODEL), D_FF ** -0.5),
        "norm_final": 1.0 + nrm(ks[20], (D_MODEL,), 0.02),
    }


def reference(x_prompt, x_sample, cache_conv, state_delta, cache_k, cache_v,
              norm_ff1, ff1_w_in, ff1_w_out, norm_mix, w_mix_in, conv_w, a_log, dt_bias,
              gnorm_w, sinks, w_mix_out, norm_ff2, ff2_w_in, ff2_w_out, norm_final):
    weights = (norm_ff1, ff1_w_in, ff1_w_out, norm_mix, w_mix_in, conv_w, a_log, dt_bias,
               gnorm_w, sinks, w_mix_out, norm_ff2, ff2_w_in, ff2_w_out, norm_final)
    bp, lp = x_prompt.shape[:2]
    zero_buf = jnp.zeros((DEPTH, bp, CONV_W - 1, A_QKV), x_prompt.dtype)
    zero_state = jnp.zeros((DEPTH, bp, A_HEADS, A_DK, A_DV), state_delta.dtype)
    pos_p = jnp.arange(lp, dtype=jnp.int32)
    y_prompt, conv_p, delta_p, k_p, v_p = trunk(x_prompt, pos_p, CHUNK, zero_buf, zero_state, None, None, *weights)
    ds = x_sample.shape[1]
    pos_s = PAST_LEN + jnp.arange(ds, dtype=jnp.int32)
    y_sample, conv_s, delta_s, k_s, v_s = trunk(x_sample, pos_s, ds, cache_conv, state_delta, cache_k, cache_v, *weights)
    return (y_prompt, y_sample, conv_p, delta_p, k_p, v_p, conv_s, delta_s, k_s, v_s)
```

```python
import functools

import numpy as np
import jax
import jax.numpy as jnp
from jax import lax
from jax.experimental import pallas as pl
from jax.experimental.pallas import tpu as pltpu

F32 = jnp.float32
BF16 = jnp.bfloat16

D_MODEL = 1024
DEPTH = 4
PAST_LEN = 4096
EPS = 1e-6
A_HEADS = 4
A_DK = 128
A_DV = 128
A_QK = A_HEADS * A_DK
A_VW = A_HEADS * A_DV
A_QKV = 2 * A_QK + A_VW
CONV_W = 4
B_HEADS = 8
B_KV_HEADS = 2
B_HD = 64
B_QW = B_HEADS * B_HD
B_KW = B_KV_HEADS * B_HD
WINDOW = 128
ROPE_DIM = B_HD // 4
ROPE_THETA = 500000.0
D_MIX = A_VW + B_QW
D_FF = 2816

LANES = 128
SUBLANES = 8
GATE_W = LANES
O_QKV = 0
O_Z = O_QKV + A_QKV
O_B = O_Z + A_VW
O_A = O_B + GATE_W
O_QB = O_A + GATE_W
O_KB = O_QB + B_QW
O_VB = O_KB + B_KW
D_IN_PACKED = O_VB + B_KW

VMEM_LIMIT = 56 * 1024 * 1024


def _params(sem, vmem=VMEM_LIMIT):
    return pltpu.CompilerParams(dimension_semantics=sem, vmem_limit_bytes=vmem)


def _resident(shape):
    nd = len(shape)
    return pl.BlockSpec(shape, lambda *_: (0,) * nd, pipeline_mode=pl.Buffered(1))


def _rms(x, w):
    return x * lax.rsqrt(jnp.mean(x * x, axis=-1, keepdims=True) + EPS) * w


def _silu(x):
    return x * jax.nn.sigmoid(x)


def _mm(a, b):
    return jnp.dot(a.astype(BF16), b.astype(BF16), preferred_element_type=F32)


def _ffn_body(x_ref, nw_ref, win_ref, wout_ref, o_ref, *, fc):
    x = x_ref[...]
    h = _rms(x, nw_ref[...]).astype(BF16)
    acc = None
    for c in range(0, D_FF, fc):
        gate = jnp.dot(h, win_ref[:, c:c + fc], preferred_element_type=F32)
        up = jnp.dot(h, win_ref[:, D_FF + c:D_FF + c + fc], preferred_element_type=F32)
        act = (_silu(gate) * up).astype(BF16)
        part = jnp.dot(act, wout_ref[c:c + fc, :], preferred_element_type=F32)
        acc = part if acc is None else acc + part
    o_ref[...] = x + 0.5 * acc


def _ffn(x, nw, w_in, w_out, *, tm, fc=1408):
    m = x.shape[0]
    return pl.pallas_call(
        functools.partial(_ffn_body, fc=fc),
        grid=(m // tm,),
        in_specs=[pl.BlockSpec((tm, D_MODEL), lambda i: (i, 0)),
                  _resident((1, D_MODEL)),
                  _resident((D_MODEL, 2 * D_FF)),
                  _resident((D_FF, D_MODEL))],
        out_specs=pl.BlockSpec((tm, D_MODEL), lambda i: (i, 0)),
        out_shape=jax.ShapeDtypeStruct((m, D_MODEL), F32),
        compiler_params=_params(("parallel",)),
        name="ffn",
    )(x, nw, w_in, w_out)


_IN_GROUPS = ((O_QKV, A_QKV), (O_Z, A_VW), (O_B, GATE_W), (O_A, GATE_W),
              (O_QB, B_QW), (O_KB, B_KW), (O_VB, B_KW))


def _proj_in_body(x_ref, nw_ref, w_ref, *out_refs):
    h = _rms(x_ref[...], nw_ref[...]).astype(BF16)
    for (off, width), o_ref in zip(_IN_GROUPS, out_refs):
        o_ref[...] = jnp.dot(h, w_ref[:, off:off + width], preferred_element_type=F32)


def _proj_in(x, nw, w, *, tm):
    m = x.shape[0]
    return pl.pallas_call(
        _proj_in_body,
        grid=(m // tm,),
        in_specs=[pl.BlockSpec((tm, D_MODEL), lambda i: (i, 0)),
                  _resident((1, D_MODEL)),
                  _resident((D_MODEL, D_IN_PACKED))],
        out_specs=[pl.BlockSpec((tm, width), lambda i: (i, 0)) for _, width in _IN_GROUPS],
        out_shape=[jax.ShapeDtypeStruct((m, width), F32) for _, width in _IN_GROUPS],
        compiler_params=_params(("parallel",)),
        name="proj_in",
    )(x, nw, w)


def _proj_out_body(x_ref, oa_ref, ob_ref, w_ref, o_ref):
    o_ref[...] = (x_ref[...]
                  + jnp.dot(oa_ref[...].astype(BF16), w_ref[0:A_VW, :], preferred_element_type=F32)
                  + jnp.dot(ob_ref[...].astype(BF16), w_ref[A_VW:D_MIX, :], preferred_element_type=F32))


def _proj_out(x, oa, ob, w, *, tm):
    m = x.shape[0]
    return pl.pallas_call(
        _proj_out_body,
        grid=(m // tm,),
        in_specs=[pl.BlockSpec((tm, D_MODEL), lambda i: (i, 0)),
                  pl.BlockSpec((tm, A_VW), lambda i: (i, 0)),
                  pl.BlockSpec((tm, B_QW), lambda i: (i, 0)),
                  _resident((D_MIX, D_MODEL))],
        out_specs=pl.BlockSpec((tm, D_MODEL), lambda i: (i, 0)),
        out_shape=jax.ShapeDtypeStruct((m, D_MODEL), F32),
        compiler_params=_params(("parallel",)),
        name="proj_out",
    )(x, oa, ob, w)


def _norm_body(x_ref, nw_ref, o_ref):
    o_ref[...] = _rms(x_ref[...], nw_ref[...])


def _final_norm(x, nw, *, tm):
    m = x.shape[0]
    return pl.pallas_call(
        _norm_body,
        grid=(m // tm,),
        in_specs=[pl.BlockSpec((tm, D_MODEL), lambda i: (i, 0)), _resident((1, D_MODEL))],
        out_specs=pl.BlockSpec((tm, D_MODEL), lambda i: (i, 0)),
        out_shape=jax.ShapeDtypeStruct((m, D_MODEL), F32),
        compiler_params=_params(("parallel",)),
        name="final_norm",
    )(x, nw)


def _chunk_cumsum(g, c):
    row = lax.broadcasted_iota(jnp.int32, g.shape, 0)
    s = 1
    while s < c:
        g = g + jnp.where(row >= s, pltpu.roll(g, s, 0), 0.0)
        s *= 2
    return g


INV_BLOCK = 32


def _unit_lower_inverse_minus_eye(l, c):
    bs = min(INV_BLOCK, c)
    row = lax.broadcasted_iota(jnp.int32, (c, c), 0)
    col = lax.broadcasted_iota(jnp.int32, (c, c), 1)
    same = lambda b: (row // b) == (col // b)
    diag = jnp.where(same(bs), l, 0.0) if bs < c else l
    p = -diag
    m = _mm(diag, diag)
    for _ in range(int(np.log2(bs)) - 2):
        p = p + m + _mm(p, m)
        m = _mm(m, m)
    p = p + m + _mm(p, m)
    b = bs
    while b < c:
        off = jnp.where(same(2 * b) & jnp.logical_not(same(b)), l, 0.0)
        x = off + _mm(p, off)
        p = p - (x + _mm(x, p))
        b *= 2
    return p


def _gdn_body(qkv_ref, z_ref, bl_ref, al_ref, cbuf_ref, s0_ref, cw_ref, alog_ref, dtb_ref, gw_ref,
              o_ref, nbuf_ref, snew_ref, xp_ref, act_ref, s_ref, *, tb, c, n_valid):
    t = pl.program_id(1)
    halo = CONV_W - 1
    base = SUBLANES

    @pl.when(t == 0)
    def _():
        xp_ref[base - halo:base, :] = cbuf_ref[...]
        s_ref[...] = s0_ref[...]

    xp_ref[base:base + tb, :] = qkv_ref[...]
    cw = cw_ref[...]
    conv = xp_ref[base - halo:base - halo + tb, :] * cw[0:1, :]
    for j in range(1, CONV_W):
        conv = conv + xp_ref[base - halo + j:base - halo + j + tb, :] * cw[j:j + 1, :]
    act_ref[...] = _silu(conv)

    @pl.when(t == pl.num_programs(1) - 1)
    def _():
        nbuf_ref[...] = xp_ref[base + n_valid - halo:base + n_valid, :]

    xp_ref[base - halo:base, :] = xp_ref[base + tb - halo:base + tb, :]

    beta_all = jax.nn.sigmoid(bl_ref[...])
    a_in = al_ref[...] + dtb_ref[...]
    softplus = jnp.maximum(a_in, 0.0) + jnp.log1p(jnp.exp(-jnp.abs(a_in)))
    g_all = -jnp.exp(alog_ref[...]) * softplus
    if n_valid < tb:
        live = lax.broadcasted_iota(jnp.int32, (tb, GATE_W), 0) < n_valid
        beta_all = jnp.where(live, beta_all, 0.0)
        g_all = jnp.where(live, g_all, 0.0)

    row = lax.broadcasted_iota(jnp.int32, (c, c), 0)
    col = lax.broadcasted_iota(jnp.int32, (c, c), 1)
    causal = row >= col
    strict = row > col
    gw = gw_ref[...]

    for ci in range(tb // c):
        r0 = ci * c
        beta = beta_all[r0:r0 + c, :]
        g_cum = _chunk_cumsum(g_all[r0:r0 + c, :], c)
        g_cum_t = g_cum.T
        g_last = g_cum[c - 1:c, :]
        e_g = jnp.exp(g_cum)
        e_gl = jnp.exp(g_last - g_cum)
        g_tot = jnp.exp(g_last)
        for h in range(A_HEADS):
            q = act_ref[r0:r0 + c, h * A_DK:(h + 1) * A_DK]
            k = act_ref[r0:r0 + c, A_QK + h * A_DK:A_QK + (h + 1) * A_DK]
            v = act_ref[r0:r0 + c, 2 * A_QK + h * A_DV:2 * A_QK + (h + 1) * A_DV]
            q = q * lax.rsqrt(jnp.sum(q * q, axis=-1, keepdims=True) + EPS) * (A_DK ** -0.5)
            k = k * lax.rsqrt(jnp.sum(k * k, axis=-1, keepdims=True) + EPS)
            b_col = beta[:, h:h + 1]
            eg_col = e_g[:, h:h + 1]
            egl_col = e_gl[:, h:h + 1]
            decay = jnp.exp(jnp.where(causal, g_cum[:, h:h + 1] - g_cum_t[h:h + 1, :], -jnp.inf))
            k16 = k.astype(BF16)
            qk_kk = lax.dot_general(jnp.concatenate([q.astype(BF16), k16], axis=0), k16,
                                    (((1,), (1,)), ((), ())), preferred_element_type=F32)
            qk = qk_kk[0:c, :] * decay
            lmat = jnp.where(strict, qk_kk[c:2 * c, :] * decay * b_col, 0.0)
            tinv = _unit_lower_inverse_minus_eye(lmat, c)
            rhs = jnp.concatenate([v * b_col, k * (b_col * eg_col)], axis=1)
            sol = rhs + _mm(tinv, rhs)
            u = sol[:, 0:A_DV]
            w = sol[:, A_DV:A_DV + A_DK]
            s_old = s_ref[h]
            wq_s = _mm(jnp.concatenate([w, q * eg_col], axis=0), s_old)
            v_new = u - wq_s[0:c, :]
            o = wq_s[c:2 * c, :] + _mm(qk, v_new)
            k_dec = (k * egl_col).astype(BF16)
            s_ref[h] = s_old * g_tot[:, h:h + 1] + lax.dot_general(
                k_dec, v_new.astype(BF16), (((0,), (0,)), ((), ())), preferred_element_type=F32)
            o = o * lax.rsqrt(jnp.mean(o * o, axis=-1, keepdims=True) + EPS) * gw
            o_ref[r0:r0 + c, h * A_DV:(h + 1) * A_DV] = o * _silu(z_ref[r0:r0 + c, h * A_DV:(h + 1) * A_DV])

    @pl.when(t == pl.num_programs(1) - 1)
    def _():
        snew_ref[...] = s_ref[...]


def _gdn(qkv, z, bl, al, cbuf, s0, cw, alog, dtb, gw, *, tb, c, n_valid):
    b, l, _ = qkv.shape
    nt = l // tb
    assert n_valid == tb or nt == 1
    blk = lambda width: pl.BlockSpec((None, tb, width), lambda i, j: (i, j, 0))
    return pl.pallas_call(
        functools.partial(_gdn_body, tb=tb, c=c, n_valid=n_valid),
        grid=(b, nt),
        in_specs=[blk(A_QKV), blk(A_VW), blk(GATE_W), blk(GATE_W),
                  pl.BlockSpec((None, CONV_W - 1, A_QKV), lambda i, j: (i, 0, 0)),
                  pl.BlockSpec((None, A_HEADS, A_DK, A_DV), lambda i, j: (i, 0, 0, 0)),
                  pl.BlockSpec((CONV_W, A_QKV), lambda i, j: (0, 0)),
                  pl.BlockSpec((1, GATE_W), lambda i, j: (0, 0)),
                  pl.BlockSpec((1, GATE_W), lambda i, j: (0, 0)),
                  pl.BlockSpec((1, A_DV), lambda i, j: (0, 0))],
        out_specs=[blk(A_VW),
                   pl.BlockSpec((None, CONV_W - 1, A_QKV), lambda i, j: (i, 0, 0)),
                   pl.BlockSpec((None, A_HEADS, A_DK, A_DV), lambda i, j: (i, 0, 0, 0))],
        out_shape=[jax.ShapeDtypeStruct((b, l, A_VW), F32),
                   jax.ShapeDtypeStruct((b, CONV_W - 1, A_QKV), F32),
                   jax.ShapeDtypeStruct((b, A_HEADS, A_DK, A_DV), F32)],
        scratch_shapes=[pltpu.VMEM((SUBLANES + tb, A_QKV), F32),
                        pltpu.VMEM((tb, A_QKV), F32),
                        pltpu.VMEM((A_HEADS, A_DK, A_DV), F32)],
        compiler_params=_params(("parallel", "arbitrary")),
        name="gdn",
    )(qkv, z, bl, al, cbuf, s0, cw, alog, dtb, gw)


def _rope_table_body(inv_ref, ma_ref, mb_ref, cos_ref, sa_ref, sb_ref, *, tm, pos0):
    pos = pos0 + pl.program_id(0) * tm + lax.broadcasted_iota(jnp.int32, (tm, LANES), 0)
    ang = pos.astype(F32) * inv_ref[...]
    sin = jnp.sin(ang)
    cos_ref[...] = jnp.cos(ang)
    sa_ref[...] = sin * ma_ref[...]
    sb_ref[...] = -sin * mb_ref[...]


def _rope_tables(inv_row, ma, mb, *, length, pos0, tm):
    row = pl.BlockSpec((1, LANES), lambda i: (0, 0))
    out = pl.BlockSpec((tm, LANES), lambda i: (i, 0))
    return pl.pallas_call(
        functools.partial(_rope_table_body, tm=tm, pos0=pos0),
        grid=(length // tm,),
        in_specs=[row, row, row],
        out_specs=[out, out, out],
        out_shape=[jax.ShapeDtypeStruct((length, LANES), F32)] * 3,
        compiler_params=_params(("parallel",)),
        name="rope_tables",
    )(inv_row, ma, mb)


def _swa_body(qb_ref, kb_ref, vb_ref, cos_ref, sa_ref, sb_ref, kc_ref, vc_ref, sinks_ref,
              o_ref, newk_ref, newv_ref, kx_ref, vx_ref, qs_ref, kvar_ref, vvar_ref,
              *, tb, c, mask_start):
    t = pl.program_id(1)
    half = B_HD
    rot = ROPE_DIM // 2
    w = WINDOW + c

    @pl.when(t == 0)
    def _():
        kx_ref[0:WINDOW, :] = kc_ref[...]
        vx_ref[0:WINDOW, :] = vc_ref[...]

    cos = cos_ref[...]
    sa = sa_ref[...]
    sb = sb_ref[...]

    def rope(x):
        return x * cos + pltpu.roll(x, rot, 1) * sa + pltpu.roll(x, LANES - rot, 1) * sb

    kx_ref[WINDOW:WINDOW + tb, :] = rope(kb_ref[...])
    vx_ref[WINDOW:WINDOW + tb, :] = vb_ref[...]
    for s in range(B_QW // LANES):
        qs_ref[:, s * LANES:(s + 1) * LANES] = (
            rope(qb_ref[:, s * LANES:(s + 1) * LANES]) * (B_HD ** -0.5)).astype(BF16)

    lo = lax.broadcasted_iota(jnp.int32, (WINDOW + tb, LANES), 1) < half
    for src, dst in ((kx_ref, kvar_ref), (vx_ref, vvar_ref)):
        full = src[...]
        swapped = pltpu.roll(full, half, 1)
        dst[0] = jnp.where(lo, full, 0.0).astype(BF16)
        dst[1] = jnp.where(lo, 0.0, swapped).astype(BF16)
        dst[2] = jnp.where(lo, swapped, 0.0).astype(BF16)
        dst[3] = jnp.where(lo, 0.0, full).astype(BF16)

    first_rows = lax.broadcasted_iota(jnp.int32, (2 * c, 1), 0) < c
    kcol = lax.broadcasted_iota(jnp.int32, (2 * c, w), 1)

    def chunk(i, carry):
        r0 = pl.multiple_of(i * c, c)
        for g in range(B_KV_HEADS):
            lhs = jnp.concatenate([qs_ref[pl.ds(r0, c), (2 * g) * LANES:(2 * g + 1) * LANES],
                                   qs_ref[pl.ds(r0, c), (2 * g + 1) * LANES:(2 * g + 2) * LANES]], axis=0)
            o = None
            for hh in range(2):
                s = lax.dot_general(lhs, kvar_ref[2 * g + hh, pl.ds(r0, w), :],
                                    (((1,), (1,)), ((), ())), preferred_element_type=F32)
                if mask_start:
                    s = jnp.where(t * tb + r0 - WINDOW + kcol >= 0, s, -jnp.inf)
                sk = jnp.where(first_rows, sinks_ref[4 * g + hh], sinks_ref[4 * g + 2 + hh])
                m = jnp.maximum(jnp.max(s, axis=-1, keepdims=True), sk)
                p = jnp.exp(s - m)
                den = jnp.sum(p, axis=-1, keepdims=True) + jnp.exp(sk - m)
                pv = jnp.dot(p.astype(BF16), vvar_ref[2 * g + hh, pl.ds(r0, w), :],
                             preferred_element_type=F32) / den
                o = pv if o is None else o + pv
            o_ref[pl.ds(r0, c), (2 * g) * LANES:(2 * g + 1) * LANES] = o[0:c, :]
            o_ref[pl.ds(r0, c), (2 * g + 1) * LANES:(2 * g + 2) * LANES] = o[c:2 * c, :]
        return carry

    lax.fori_loop(0, tb // c, chunk, 0)

    @pl.when(t == pl.num_programs(1) - 1)
    def _():
        newk_ref[...] = kx_ref[tb:tb + WINDOW, :]
        newv_ref[...] = vx_ref[tb:tb + WINDOW, :]

    if tb >= WINDOW:
        kx_ref[0:WINDOW, :] = kx_ref[tb:tb + WINDOW, :]
        vx_ref[0:WINDOW, :] = vx_ref[tb:tb + WINDOW, :]


def _swa(qb, kb, vb, cos, sa, sb, kc, vc, sinks, *, tb, c, mask_start):
    b, l, _ = qb.shape
    nt = l // tb
    assert tb >= WINDOW or nt == 1
    blk = lambda width: pl.BlockSpec((None, tb, width), lambda i, j: (i, j, 0))
    tab = pl.BlockSpec((tb, LANES), lambda i, j: (j, 0))
    cache = pl.BlockSpec((None, WINDOW, B_KW), lambda i, j: (i, 0, 0))
    return pl.pallas_call(
        functools.partial(_swa_body, tb=tb, c=c, mask_start=mask_start),
        grid=(b, nt),
        in_specs=[blk(B_QW), blk(B_KW), blk(B_KW), tab, tab, tab, cache, cache,
                  pl.BlockSpec(memory_space=pltpu.SMEM)],
        out_specs=[blk(B_QW), cache, cache],
        out_shape=[jax.ShapeDtypeStruct((b, l, B_QW), F32),
                   jax.ShapeDtypeStruct((b, WINDOW, B_KW), F32),
                   jax.ShapeDtypeStruct((b, WINDOW, B_KW), F32)],
        scratch_shapes=[pltpu.VMEM((WINDOW + tb, B_KW), F32),
                        pltpu.VMEM((WINDOW + tb, B_KW), F32),
                        pltpu.VMEM((tb, B_QW), BF16),
                        pltpu.VMEM((4, WINDOW + tb, LANES), BF16),
                        pltpu.VMEM((4, WINDOW + tb, LANES), BF16)],
        compiler_params=_params(("parallel", "arbitrary")),
        name="swa",
    )(qb, kb, vb, cos, sa, sb, kc, vc, sinks)


def _pack_w_in(w):
    offs = np.cumsum([0, A_QKV, A_VW, A_HEADS, A_HEADS, B_QW, B_KW, B_KW]).tolist()
    qkv, z, bg, ag, qb, kb, vb = [w[..., offs[i]:offs[i + 1]] for i in range(7)]
    pad = lambda g: jnp.pad(g, ((0, 0), (0, 0), (0, GATE_W - g.shape[-1])))
    return jnp.concatenate([qkv, z, pad(bg), pad(ag), qb, kb, vb], axis=-1).astype(BF16)


def _pad_lanes(v):
    return jnp.pad(v, ((0, 0), (0, GATE_W - v.shape[-1])))[:, None, :]


def _trunk(x, rope_tabs, conv_bufs, s0s, k_caches, v_caches, wts, *, batch, length, tm,
           gdn_tb, gdn_c, swa_tb, swa_c, mask_start):
    (norm_ff1, ff1_in, ff1_out, norm_mix, w_in, conv_w, alog, dtb, gnorm, sinks, w_out,
     norm_ff2, ff2_in, ff2_out, norm_final) = wts
    cos, sa, sb = rope_tabs
    lpad = -(-length // gdn_tb) * gdn_tb
    bufs, states, ks, vs = [], [], [], []

    def seq(a, width):
        return a.reshape(batch, length, width)

    def seq_pad(a, width):
        a = seq(a, width)
        return a if lpad == length else jnp.pad(a, ((0, 0), (0, lpad - length), (0, 0)))

    for l in range(DEPTH):
        x = _ffn(x, norm_ff1[l], ff1_in[l], ff1_out[l], tm=tm)
        qkv, z, bg, ag, qb, kb, vb = _proj_in(x, norm_mix[l], w_in[l], tm=tm)
        o_a, nbuf, s_new = _gdn(seq_pad(qkv, A_QKV), seq_pad(z, A_VW), seq_pad(bg, GATE_W), seq_pad(ag, GATE_W),
                                conv_bufs[l], s0s[l], conv_w[l], alog[l], dtb[l], gnorm[l],
                                tb=gdn_tb, c=gdn_c, n_valid=min(length, gdn_tb))
        o_b, nk, nv = _swa(seq(qb, B_QW), seq(kb, B_KW), seq(vb, B_KW), cos, sa, sb,
                           k_caches[l], v_caches[l], sinks[l], tb=swa_tb, c=swa_c, mask_start=mask_start)
        o_a = o_a[:, :length].reshape(batch * length, A_VW)
        x = _proj_out(x, o_a, o_b.reshape(batch * length, B_QW), w_out[l], tm=tm)
        x = _ffn(x, norm_ff2[l], ff2_in[l], ff2_out[l], tm=tm)
        bufs.append(nbuf)
        states.append(s_new)
        ks.append(nk.reshape(batch, WINDOW, B_KV_HEADS, B_HD))
        vs.append(nv.reshape(batch, WINDOW, B_KV_HEADS, B_HD))
    y = _final_norm(x, norm_final, tm=tm).reshape(batch, length, D_MODEL)
    return y, jnp.stack(bufs), jnp.stack(states), jnp.stack(ks), jnp.stack(vs)


def kernel(x_prompt, x_sample, cache_conv, state_delta, cache_k, cache_v, norm_ff1, ff1_w_in, ff1_w_out, norm_mix, w_mix_in, conv_w, a_log, dt_bias, gnorm_w, sinks, w_mix_out, norm_ff2, ff2_w_in, ff2_w_out, norm_final):
    bp, lp, _ = x_prompt.shape
    bs, ls, _ = x_sample.shape
    rows = cache_k.shape[2]
    assert rows == WINDOW

    wts = (norm_ff1[:, None, :], ff1_w_in.astype(BF16), ff1_w_out.astype(BF16), norm_mix[:, None, :],
           _pack_w_in(w_mix_in), conv_w, _pad_lanes(a_log), _pad_lanes(dt_bias), gnorm_w[:, None, :], sinks,
           w_mix_out.astype(BF16), norm_ff2[:, None, :], ff2_w_in.astype(BF16), ff2_w_out.astype(BF16),
           norm_final[None, :])

    inv = jnp.power(ROPE_THETA, -jnp.arange(0, ROPE_DIM, 2, dtype=F32) / ROPE_DIM)
    rot = ROPE_DIM // 2
    head_row = jnp.concatenate([inv, inv, jnp.zeros((B_HD - ROPE_DIM,), F32)])
    inv_row = jnp.tile(head_row, LANES // B_HD)[None, :]
    d = np.arange(LANES) % B_HD
    ma = jnp.asarray(((d >= rot) & (d < ROPE_DIM)).astype(np.float32))[None, :]
    mb = jnp.asarray((d < rot).astype(np.float32))[None, :]

    zero_buf = jnp.zeros((DEPTH, bp, CONV_W - 1, A_QKV), F32)
    zero_state = jnp.zeros((DEPTH, bp, A_HEADS, A_DK, A_DV), F32)
    zero_kv = jnp.zeros((DEPTH, bp, WINDOW, B_KW), F32)
    tabs_p = _rope_tables(inv_row, ma, mb, length=lp, pos0=0, tm=512)
    y_p, conv_p, delta_p, k_p, v_p = _trunk(
        x_prompt.reshape(bp * lp, D_MODEL), tabs_p, zero_buf, zero_state, zero_kv, zero_kv, wts,
        batch=bp, length=lp, tm=512, gdn_tb=512, gdn_c=128, swa_tb=512, swa_c=64, mask_start=True)

    tabs_s = _rope_tables(inv_row, ma, mb, length=ls, pos0=PAST_LEN, tm=ls)
    y_s, conv_s, delta_s, k_s, v_s = _trunk(
        x_sample.reshape(bs * ls, D_MODEL), tabs_s, cache_conv, state_delta,
        cache_k.reshape(DEPTH, bs, WINDOW, B_KW), cache_v.reshape(DEPTH, bs, WINDOW, B_KW), wts,
        batch=bs, length=ls, tm=bs * ls, gdn_tb=128, gdn_c=128, swa_tb=ls, swa_c=ls, mask_start=False)
    return (y_p, y_s, conv_p, delta_p, k_p, v_p, conv_s, delta_s, k_s, v_s)
```

```python
import functools

import numpy as np
import jax
import jax.numpy as jnp
from jax import lax
from jax.experimental import pallas as pl
from jax.experimental.pallas import tpu as pltpu

F32 = jnp.float32
BF16 = jnp.bfloat16

D_MODEL = 1024
DEPTH = 4
PAST_LEN = 4096
EPS = 1e-6
A_HEADS = 4
A_DK = 128
A_DV = 128
A_QK = A_HEADS * A_DK
A_VW = A_HEADS * A_DV
A_QKV = 2 * A_QK + A_VW
CONV_W = 4
B_HEADS = 8
B_KV_HEADS = 2
B_HD = 64
B_QW = B_HEADS * B_HD
B_KW = B_KV_HEADS * B_HD
WINDOW = 128
ROPE_DIM = B_HD // 4
ROPE_THETA = 500000.0
D_MIX = A_VW + B_QW
D_FF = 2816

LANES = 128
SUBLANES = 8
GATE_W = LANES
O_QKV = 0
O_Z = O_QKV + A_QKV
O_B = O_Z + A_VW
O_A = O_B + GATE_W
O_QB = O_A + GATE_W
O_KB = O_QB + B_QW
O_VB = O_KB + B_KW
D_IN_PACKED = O_VB + B_KW

VMEM_LIMIT = 56 * 1024 * 1024


def _params(sem, vmem=VMEM_LIMIT):
    return pltpu.CompilerParams(dimension_semantics=sem, vmem_limit_bytes=vmem)


def _resident(shape):
    nd = len(shape)
    return pl.BlockSpec(shape, lambda *_: (0,) * nd, pipeline_mode=pl.Buffered(1))


def _rms(x, w):
    return x * lax.rsqrt(jnp.mean(x * x, axis=-1, keepdims=True) + EPS) * w


def _silu(x):
    return x * jax.nn.sigmoid(x)


def _mm(a, b):
    return jnp.dot(a.astype(BF16), b.astype(BF16), preferred_element_type=F32)


def _ffn_body(x_ref, nw_ref, win_ref, wout_ref, o_ref, *, fc):
    x = x_ref[...]
    h = _rms(x, nw_ref[...]).astype(BF16)
    acc = None
    for c in range(0, D_FF, fc):
        gate = jnp.dot(h, win_ref[:, c:c + fc], preferred_element_type=F32)
        up = jnp.dot(h, win_ref[:, D_FF + c:D_FF + c + fc], preferred_element_type=F32)
        act = (_silu(gate) * up).astype(BF16)
        part = jnp.dot(act, wout_ref[c:c + fc, :], preferred_element_type=F32)
        acc = part if acc is None else acc + part
    o_ref[...] = x + 0.5 * acc


def _ffn(x, nw, w_in, w_out, *, tm, fc=1408):
    m = x.shape[0]
    return pl.pallas_call(
        functools.partial(_ffn_body, fc=fc),
        grid=(m // tm,),
        in_specs=[pl.BlockSpec((tm, D_MODEL), lambda i: (i, 0)),
                  _resident((1, D_MODEL)),
                  _resident((D_MODEL, 2 * D_FF)),
                  _resident((D_FF, D_MODEL))],
        out_specs=pl.BlockSpec((tm, D_MODEL), lambda i: (i, 0)),
        out_shape=jax.ShapeDtypeStruct((m, D_MODEL), F32),
        compiler_params=_params(("parallel",)),
        name="ffn",
    )(x, nw, w_in, w_out)


_IN_GROUPS = ((O_QKV, A_QKV), (O_Z, A_VW), (O_B, GATE_W), (O_A, GATE_W),
              (O_QB, B_QW), (O_KB, B_KW), (O_VB, B_KW))


def _proj_in_body(x_ref, nw_ref, w_ref, *out_refs):
    h = _rms(x_ref[...], nw_ref[...]).astype(BF16)
    for (off, width), o_ref in zip(_IN_GROUPS, out_refs):
        o_ref[...] = jnp.dot(h, w_ref[:, off:off + width], preferred_element_type=F32)


def _proj_in(x, nw, w, *, tm):
    m = x.shape[0]
    return pl.pallas_call(
        _proj_in_body,
        grid=(m // tm,),
        in_specs=[pl.BlockSpec((tm, D_MODEL), lambda i: (i, 0)),
                  _resident((1, D_MODEL)),
                  _resident((D_MODEL, D_IN_PACKED))],
        out_specs=[pl.BlockSpec((tm, width), lambda i: (i, 0)) for _, width in _IN_GROUPS],
        out_shape=[jax.ShapeDtypeStruct((m, width), F32) for _, width in _IN_GROUPS],
        compiler_params=_params(("parallel",)),
        name="proj_in",
    )(x, nw, w)


def _proj_out_body(x_ref, oa_ref, ob_ref, w_ref, o_ref):
    o_ref[...] = (x_ref[...]
                  + jnp.dot(oa_ref[...].astype(BF16), w_ref[0:A_VW, :], preferred_element_type=F32)
                  + jnp.dot(ob_ref[...].astype(BF16), w_ref[A_VW:D_MIX, :], preferred_element_type=F32))


def _proj_out(x, oa, ob, w, *, tm):
    m = x.shape[0]
    return pl.pallas_call(
        _proj_out_body,
        grid=(m // tm,),
        in_specs=[pl.BlockSpec((tm, D_MODEL), lambda i: (i, 0)),
                  pl.BlockSpec((tm, A_VW), lambda i: (i, 0)),
                  pl.BlockSpec((tm, B_QW), lambda i: (i, 0)),
                  _resident((D_MIX, D_MODEL))],
        out_specs=pl.BlockSpec((tm, D_MODEL), lambda i: (i, 0)),
        out_shape=jax.ShapeDtypeStruct((m, D_MODEL), F32),
        compiler_params=_params(("parallel",)),
        name="proj_out",
    )(x, oa, ob, w)


def _norm_body(x_ref, nw_ref, o_ref):
    o_ref[...] = _rms(x_ref[...], nw_ref[...])


def _final_norm(x, nw, *, tm):
    m = x.shape[0]
    return pl.pallas_call(
        _norm_body,
        grid=(m // tm,),
        in_specs=[pl.BlockSpec((tm, D_MODEL), lambda i: (i, 0)), _resident((1, D_MODEL))],
        out_specs=pl.BlockSpec((tm, D_MODEL), lambda i: (i, 0)),
        out_shape=jax.ShapeDtypeStruct((m, D_MODEL), F32),
        compiler_params=_params(("parallel",)),
        name="final_norm",
    )(x, nw)


def _chunk_cumsum(g, c):
    row = lax.broadcasted_iota(jnp.int32, g.shape, 0)
    s = 1
    while s < c:
        g = g + jnp.where(row >= s, pltpu.roll(g, s, 0), 0.0)
        s *= 2
    return g


INV_BLOCK = 32


def _unit_lower_inverses_minus_eye(ls, c):
    bs = min(INV_BLOCK, c)
    row = lax.broadcasted_iota(jnp.int32, (c, c), 0)
    col = lax.broadcasted_iota(jnp.int32, (c, c), 1)
    same = lambda b: (row // b) == (col // b)
    diag = [jnp.where(same(bs), l, 0.0) for l in ls] if bs < c else ls
    p = [-d for d in diag]
    m = [_mm(d, d) for d in diag]
    for _ in range(int(np.log2(bs)) - 2):
        pm = [_mm(pi, mi) for pi, mi in zip(p, m)]
        p = [pi + mi + pmi for pi, mi, pmi in zip(p, m, pm)]
        m = [_mm(mi, mi) for mi in m]
    pm = [_mm(pi, mi) for pi, mi in zip(p, m)]
    p = [pi + mi + pmi for pi, mi, pmi in zip(p, m, pm)]
    b = bs
    while b < c:
        sel = same(2 * b) & jnp.logical_not(same(b))
        off = [jnp.where(sel, l, 0.0) for l in ls]
        x = [oi + _mm(pi, oi) for pi, oi in zip(p, off)]
        p = [pi - (xi + _mm(xi, pi)) for pi, xi in zip(p, x)]
        b *= 2
    return p


def _gdn_body(qkv_ref, z_ref, bl_ref, al_ref, cbuf_ref, s0_ref, cw_ref, alog_ref, dtb_ref, gw_ref,
              o_ref, nbuf_ref, snew_ref, xp_ref, act_ref, s_ref, u_ref, wq_ref, qk_ref, kdec_ref,
              *, tb, c, n_valid):
    t = pl.program_id(1)
    halo = CONV_W - 1
    base = SUBLANES

    @pl.when(t == 0)
    def _():
        xp_ref[base - halo:base, :] = cbuf_ref[...]
        s_ref[...] = s0_ref[...]

    xp_ref[base:base + tb, :] = qkv_ref[...]
    cw = cw_ref[...]
    conv = xp_ref[base - halo:base - halo + tb, :] * cw[0:1, :]
    for j in range(1, CONV_W):
        conv = conv + xp_ref[base - halo + j:base - halo + j + tb, :] * cw[j:j + 1, :]
    act_ref[...] = _silu(conv)

    @pl.when(t == pl.num_programs(1) - 1)
    def _():
        nbuf_ref[...] = xp_ref[base + n_valid - halo:base + n_valid, :]

    xp_ref[base - halo:base, :] = xp_ref[base + tb - halo:base + tb, :]

    beta_all = jax.nn.sigmoid(bl_ref[...])
    a_in = al_ref[...] + dtb_ref[...]
    softplus = jnp.maximum(a_in, 0.0) + jnp.log1p(jnp.exp(-jnp.abs(a_in)))
    g_all = -jnp.exp(alog_ref[...]) * softplus
    if n_valid < tb:
        live = lax.broadcasted_iota(jnp.int32, (tb, GATE_W), 0) < n_valid
        beta_all = jnp.where(live, beta_all, 0.0)
        g_all = jnp.where(live, g_all, 0.0)

    row = lax.broadcasted_iota(jnp.int32, (c, c), 0)
    col = lax.broadcasted_iota(jnp.int32, (c, c), 1)
    causal = row >= col
    strict = row > col
    gw = gw_ref[...]
    n_chunks = tb // c

    g_tots = []
    lmats, rhss = [], []
    for ci in range(n_chunks):
        r0 = ci * c
        beta = beta_all[r0:r0 + c, :]
        g_cum = _chunk_cumsum(g_all[r0:r0 + c, :], c)
        g_cum_t = g_cum.T
        g_last = g_cum[c - 1:c, :]
        e_g = jnp.exp(g_cum)
        e_gl = jnp.exp(g_last - g_cum)
        g_tots.append(jnp.exp(g_last))
        for h in range(A_HEADS):
            i = ci * A_HEADS + h
            q = act_ref[r0:r0 + c, h * A_DK:(h + 1) * A_DK]
            k = act_ref[r0:r0 + c, A_QK + h * A_DK:A_QK + (h + 1) * A_DK]
            v = act_ref[r0:r0 + c, 2 * A_QK + h * A_DV:2 * A_QK + (h + 1) * A_DV]
            q = q * lax.rsqrt(jnp.sum(q * q, axis=-1, keepdims=True) + EPS) * (A_DK ** -0.5)
            k = k * lax.rsqrt(jnp.sum(k * k, axis=-1, keepdims=True) + EPS)
            b_col = beta[:, h:h + 1]
            eg_col = e_g[:, h:h + 1]
            decay = jnp.exp(jnp.where(causal, g_cum[:, h:h + 1] - g_cum_t[h:h + 1, :], -jnp.inf))
            k16 = k.astype(BF16)
            qk_kk = lax.dot_general(jnp.concatenate([q.astype(BF16), k16], axis=0), k16,
                                    (((1,), (1,)), ((), ())), preferred_element_type=F32)
            qk_ref[i] = (qk_kk[0:c, :] * decay).astype(BF16)
            lmats.append(jnp.where(strict, qk_kk[c:2 * c, :] * decay * b_col, 0.0))
            rhss.append(jnp.concatenate([v * b_col, k * (b_col * eg_col)], axis=1))
            wq_ref[i, c:2 * c, :] = (q * eg_col).astype(BF16)
            kdec_ref[i] = (k * e_gl[:, h:h + 1]).astype(BF16)
    tinvs = _unit_lower_inverses_minus_eye(lmats, c)
    for i, (tinv, rhs) in enumerate(zip(tinvs, rhss)):
        sol = rhs + _mm(tinv, rhs)
        u_ref[i] = sol[:, 0:A_DV]
        wq_ref[i, 0:c, :] = sol[:, A_DV:A_DV + A_DK].astype(BF16)

    for ci in range(n_chunks):
        r0 = ci * c
        idx = [ci * A_HEADS + h for h in range(A_HEADS)]
        s_old = [s_ref[h] for h in range(A_HEADS)]
        wq_s = [jnp.dot(wq_ref[i], s.astype(BF16), preferred_element_type=F32) for i, s in zip(idx, s_old)]
        v_new = [u_ref[i] - x[0:c, :] for i, x in zip(idx, wq_s)]
        v16 = [x.astype(BF16) for x in v_new]
        for h in range(A_HEADS):
            s_ref[h] = s_old[h] * g_tots[ci][:, h:h + 1] + lax.dot_general(
                kdec_ref[idx[h]], v16[h], (((0,), (0,)), ((), ())), preferred_element_type=F32)
        for h in range(A_HEADS):
            o = wq_s[h][c:2 * c, :] + jnp.dot(qk_ref[idx[h]], v16[h], preferred_element_type=F32)
            o = o * lax.rsqrt(jnp.mean(o * o, axis=-1, keepdims=True) + EPS) * gw
            o_ref[r0:r0 + c, h * A_DV:(h + 1) * A_DV] = o * _silu(z_ref[r0:r0 + c, h * A_DV:(h + 1) * A_DV])

    @pl.when(t == pl.num_programs(1) - 1)
    def _():
        snew_ref[...] = s_ref[...]


def _gdn(qkv, z, bl, al, cbuf, s0, cw, alog, dtb, gw, *, tb, c, n_valid):
    b, l, _ = qkv.shape
    nt = l // tb
    assert n_valid == tb or nt == 1
    n_items = (tb // c) * A_HEADS
    blk = lambda width: pl.BlockSpec((None, tb, width), lambda i, j: (i, j, 0))
    return pl.pallas_call(
        functools.partial(_gdn_body, tb=tb, c=c, n_valid=n_valid),
        grid=(b, nt),
        in_specs=[blk(A_QKV), blk(A_VW), blk(GATE_W), blk(GATE_W),
                  pl.BlockSpec((None, CONV_W - 1, A_QKV), lambda i, j: (i, 0, 0)),
                  pl.BlockSpec((None, A_HEADS, A_DK, A_DV), lambda i, j: (i, 0, 0, 0)),
                  pl.BlockSpec((CONV_W, A_QKV), lambda i, j: (0, 0)),
                  pl.BlockSpec((1, GATE_W), lambda i, j: (0, 0)),
                  pl.BlockSpec((1, GATE_W), lambda i, j: (0, 0)),
                  pl.BlockSpec((1, A_DV), lambda i, j: (0, 0))],
        out_specs=[blk(A_VW),
                   pl.BlockSpec((None, CONV_W - 1, A_QKV), lambda i, j: (i, 0, 0)),
                   pl.BlockSpec((None, A_HEADS, A_DK, A_DV), lambda i, j: (i, 0, 0, 0))],
        out_shape=[jax.ShapeDtypeStruct((b, l, A_VW), F32),
                   jax.ShapeDtypeStruct((b, CONV_W - 1, A_QKV), F32),
                   jax.ShapeDtypeStruct((b, A_HEADS, A_DK, A_DV), F32)],
        scratch_shapes=[pltpu.VMEM((SUBLANES + tb, A_QKV), F32),
                        pltpu.VMEM((tb, A_QKV), F32),
                        pltpu.VMEM((A_HEADS, A_DK, A_DV), F32),
                        pltpu.VMEM((n_items, c, A_DV), F32),
                        pltpu.VMEM((n_items, 2 * c, A_DK), BF16),
                        pltpu.VMEM((n_items, c, c), BF16),
                        pltpu.VMEM((n_items, c, A_DK), BF16)],
        compiler_params=_params(("parallel", "arbitrary")),
        name="gdn",
    )(qkv, z, bl, al, cbuf, s0, cw, alog, dtb, gw)


def _rope_table_body(inv_ref, ma_ref, mb_ref, cos_ref, sa_ref, sb_ref, *, tm, pos0):
    pos = pos0 + pl.program_id(0) * tm + lax.broadcasted_iota(jnp.int32, (tm, LANES), 0)
    ang = pos.astype(F32) * inv_ref[...]
    sin = jnp.sin(ang)
    cos_ref[...] = jnp.cos(ang)
    sa_ref[...] = sin * ma_ref[...]
    sb_ref[...] = -sin * mb_ref[...]


def _rope_tables(inv_row, ma, mb, *, length, pos0, tm):
    row = pl.BlockSpec((1, LANES), lambda i: (0, 0))
    out = pl.BlockSpec((tm, LANES), lambda i: (i, 0))
    return pl.pallas_call(
        functools.partial(_rope_table_body, tm=tm, pos0=pos0),
        grid=(length // tm,),
        in_specs=[row, row, row],
        out_specs=[out, out, out],
        out_shape=[jax.ShapeDtypeStruct((length, LANES), F32)] * 3,
        compiler_params=_params(("parallel",)),
        name="rope_tables",
    )(inv_row, ma, mb)


def _swa_body(qb_ref, kb_ref, vb_ref, cos_ref, sa_ref, sb_ref, kc_ref, vc_ref, sinks_ref,
              o_ref, newk_ref, newv_ref, kx_ref, vx_ref, qs_ref, kvar_ref, vvar_ref,
              *, tb, c, mask_start):
    t = pl.program_id(1)
    half = B_HD
    rot = ROPE_DIM // 2
    w = WINDOW + c

    @pl.when(t == 0)
    def _():
        kx_ref[0:WINDOW, :] = kc_ref[...]
        vx_ref[0:WINDOW, :] = vc_ref[...]

    cos = cos_ref[...]
    sa = sa_ref[...]
    sb = sb_ref[...]

    def rope(x):
        return x * cos + pltpu.roll(x, rot, 1) * sa + pltpu.roll(x, LANES - rot, 1) * sb

    kx_ref[WINDOW:WINDOW + tb, :] = rope(kb_ref[...])
    vx_ref[WINDOW:WINDOW + tb, :] = vb_ref[...]
    for s in range(B_QW // LANES):
        qs_ref[:, s * LANES:(s + 1) * LANES] = (
            rope(qb_ref[:, s * LANES:(s + 1) * LANES]) * (B_HD ** -0.5)).astype(BF16)

    lo = lax.broadcasted_iota(jnp.int32, (WINDOW + tb, LANES), 1) < half
    for src, dst in ((kx_ref, kvar_ref), (vx_ref, vvar_ref)):
        full = src[...]
        swapped = pltpu.roll(full, half, 1)
        dst[0] = jnp.where(lo, full, 0.0).astype(BF16)
        dst[1] = jnp.where(lo, 0.0, swapped).astype(BF16)
        dst[2] = jnp.where(lo, swapped, 0.0).astype(BF16)
        dst[3] = jnp.where(lo, 0.0, full).astype(BF16)

    first_rows = lax.broadcasted_iota(jnp.int32, (2 * c, 1), 0) < c
    kcol = lax.broadcasted_iota(jnp.int32, (2 * c, w), 1)

    def chunk(i, carry):
        r0 = pl.multiple_of(i * c, c)
        for g in range(B_KV_HEADS):
            lhs = jnp.concatenate([qs_ref[pl.ds(r0, c), (2 * g) * LANES:(2 * g + 1) * LANES],
                                   qs_ref[pl.ds(r0, c), (2 * g + 1) * LANES:(2 * g + 2) * LANES]], axis=0)
            o = None
            for hh in range(2):
                s = lax.dot_general(lhs, kvar_ref[2 * g + hh, pl.ds(r0, w), :],
                                    (((1,), (1,)), ((), ())), preferred_element_type=F32)
                if mask_start:
                    s = jnp.where(t * tb + r0 - WINDOW + kcol >= 0, s, -jnp.inf)
                sk = jnp.where(first_rows, sinks_ref[4 * g + hh], sinks_ref[4 * g + 2 + hh])
                m = jnp.maximum(jnp.max(s, axis=-1, keepdims=True), sk)
                p = jnp.exp(s - m)
                den = jnp.sum(p, axis=-1, keepdims=True) + jnp.exp(sk - m)
                pv = jnp.dot(p.astype(BF16), vvar_ref[2 * g + hh, pl.ds(r0, w), :],
                             preferred_element_type=F32) / den
                o = pv if o is None else o + pv
            o_ref[pl.ds(r0, c), (2 * g) * LANES:(2 * g + 1) * LANES] = o[0:c, :]
            o_ref[pl.ds(r0, c), (2 * g + 1) * LANES:(2 * g + 2) * LANES] = o[c:2 * c, :]
        return carry

    lax.fori_loop(0, tb // c, chunk, 0)

    @pl.when(t == pl.num_programs(1) - 1)
    def _():
        newk_ref[...] = kx_ref[tb:tb + WINDOW, :]
        newv_ref[...] = vx_ref[tb:tb + WINDOW, :]

    if tb >= WINDOW:
        kx_ref[0:WINDOW, :] = kx_ref[tb:tb + WINDOW, :]
        vx_ref[0:WINDOW, :] = vx_ref[tb:tb + WINDOW, :]


def _swa(qb, kb, vb, cos, sa, sb, kc, vc, sinks, *, tb, c, mask_start):
    b, l, _ = qb.shape
    nt = l // tb
    assert tb >= WINDOW or nt == 1
    blk = lambda width: pl.BlockSpec((None, tb, width), lambda i, j: (i, j, 0))
    tab = pl.BlockSpec((tb, LANES), lambda i, j: (j, 0))
    cache = pl.BlockSpec((None, WINDOW, B_KW), lambda i, j: (i, 0, 0))
    return pl.pallas_call(
        functools.partial(_swa_body, tb=tb, c=c, mask_start=mask_start),
        grid=(b, nt),
        in_specs=[blk(B_QW), blk(B_KW), blk(B_KW), tab, tab, tab, cache, cache,
                  pl.BlockSpec(memory_space=pltpu.SMEM)],
        out_specs=[blk(B_QW), cache, cache],
        out_shape=[jax.ShapeDtypeStruct((b, l, B_QW), F32),
                   jax.ShapeDtypeStruct((b, WINDOW, B_KW), F32),
                   jax.ShapeDtypeStruct((b, WINDOW, B_KW), F32)],
        scratch_shapes=[pltpu.VMEM((WINDOW + tb, B_KW), F32),
                        pltpu.VMEM((WINDOW + tb, B_KW), F32),
                        pltpu.VMEM((tb, B_QW), BF16),
                        pltpu.VMEM((4, WINDOW + tb, LANES), BF16),
                        pltpu.VMEM((4, WINDOW + tb, LANES), BF16)],
        compiler_params=_params(("parallel", "arbitrary")),
        name="swa",
    )(qb, kb, vb, cos, sa, sb, kc, vc, sinks)


def _pack_w_in(w):
    offs = np.cumsum([0, A_QKV, A_VW, A_HEADS, A_HEADS, B_QW, B_KW, B_KW]).tolist()
    qkv, z, bg, ag, qb, kb, vb = [w[..., offs[i]:offs[i + 1]] for i in range(7)]
    pad = lambda g: jnp.pad(g, ((0, 0), (0, 0), (0, GATE_W - g.shape[-1])))
    return jnp.concatenate([qkv, z, pad(bg), pad(ag), qb, kb, vb], axis=-1).astype(BF16)


def _pad_lanes(v):
    return jnp.pad(v, ((0, 0), (0, GATE_W - v.shape[-1])))[:, None, :]


def _trunk(x, rope_tabs, conv_bufs, s0s, k_caches, v_caches, wts, *, batch, length, tm,
           gdn_tb, gdn_c, swa_tb, swa_c, mask_start):
    (norm_ff1, ff1_in, ff1_out, norm_mix, w_in, conv_w, alog, dtb, gnorm, sinks, w_out,
     norm_ff2, ff2_in, ff2_out, norm_final) = wts
    cos, sa, sb = rope_tabs
    lpad = -(-length // gdn_tb) * gdn_tb
    bufs, states, ks, vs = [], [], [], []

    def seq(a, width):
        return a.reshape(batch, length, width)

    def seq_pad(a, width):
        a = seq(a, width)
        return a if lpad == length else jnp.pad(a, ((0, 0), (0, lpad - length), (0, 0)))

    for l in range(DEPTH):
        x = _ffn(x, norm_ff1[l], ff1_in[l], ff1_out[l], tm=tm)
        qkv, z, bg, ag, qb, kb, vb = _proj_in(x, norm_mix[l], w_in[l], tm=tm)
        o_a, nbuf, s_new = _gdn(seq_pad(qkv, A_QKV), seq_pad(z, A_VW), seq_pad(bg, GATE_W), seq_pad(ag, GATE_W),
                                conv_bufs[l], s0s[l], conv_w[l], alog[l], dtb[l], gnorm[l],
                                tb=gdn_tb, c=gdn_c, n_valid=min(length, gdn_tb))
        o_b, nk, nv = _swa(seq(qb, B_QW), seq(kb, B_KW), seq(vb, B_KW), cos, sa, sb,
                           k_caches[l], v_caches[l], sinks[l], tb=swa_tb, c=swa_c, mask_start=mask_start)
        o_a = o_a[:, :length].reshape(batch * length, A_VW)
        x = _proj_out(x, o_a, o_b.reshape(batch * length, B_QW), w_out[l], tm=tm)
        x = _ffn(x, norm_ff2[l], ff2_in[l], ff2_out[l], tm=tm)
        bufs.append(nbuf)
        states.append(s_new)
        ks.append(nk.reshape(batch, WINDOW, B_KV_HEADS, B_HD))
        vs.append(nv.reshape(batch, WINDOW, B_KV_HEADS, B_HD))
    y = _final_norm(x, norm_final, tm=tm).reshape(batch, length, D_MODEL)
    return y, jnp.stack(bufs), jnp.stack(states), jnp.stack(ks), jnp.stack(vs)


def kernel(x_prompt, x_sample, cache_conv, state_delta, cache_k, cache_v, norm_ff1, ff1_w_in, ff1_w_out, norm_mix, w_mix_in, conv_w, a_log, dt_bias, gnorm_w, sinks, w_mix_out, norm_ff2, ff2_w_in, ff2_w_out, norm_final):
    bp, lp, _ = x_prompt.shape
    bs, ls, _ = x_sample.shape
    rows = cache_k.shape[2]
    assert rows == WINDOW

    wts = (norm_ff1[:, None, :], ff1_w_in.astype(BF16), ff1_w_out.astype(BF16), norm_mix[:, None, :],
           _pack_w_in(w_mix_in), conv_w, _pad_lanes(a_log), _pad_lanes(dt_bias), gnorm_w[:, None, :], sinks,
           w_mix_out.astype(BF16), norm_ff2[:, None, :], ff2_w_in.astype(BF16), ff2_w_out.astype(BF16),
           norm_final[None, :])

    inv = jnp.power(ROPE_THETA, -jnp.arange(0, ROPE_DIM, 2, dtype=F32) / ROPE_DIM)
    rot = ROPE_DIM // 2
    head_row = jnp.concatenate([inv, inv, jnp.zeros((B_HD - ROPE_DIM,), F32)])
    inv_row = jnp.tile(head_row, LANES // B_HD)[None, :]
    d = np.arange(LANES) % B_HD
    ma = jnp.asarray(((d >= rot) & (d < ROPE_DIM)).astype(np.float32))[None, :]
    mb = jnp.asarray((d < rot).astype(np.float32))[None, :]

    zero_buf = jnp.zeros((DEPTH, bp, CONV_W - 1, A_QKV), F32)
    zero_state = jnp.zeros((DEPTH, bp, A_HEADS, A_DK, A_DV), F32)
    zero_kv = jnp.zeros((DEPTH, bp, WINDOW, B_KW), F32)
    tabs_p = _rope_tables(inv_row, ma, mb, length=lp, pos0=0, tm=512)
    y_p, conv_p, delta_p, k_p, v_p = _trunk(
        x_prompt.reshape(bp * lp, D_MODEL), tabs_p, zero_buf, zero_state, zero_kv, zero_kv, wts,
        batch=bp, length=lp, tm=512, gdn_tb=512, gdn_c=128, swa_tb=512, swa_c=64, mask_start=True)

    tabs_s = _rope_tables(inv_row, ma, mb, length=ls, pos0=PAST_LEN, tm=ls)
    y_s, conv_s, delta_s, k_s, v_s = _trunk(
        x_sample.reshape(bs * ls, D_MODEL), tabs_s, cache_conv, state_delta,
        cache_k.reshape(DEPTH, bs, WINDOW, B_KW), cache_v.reshape(DEPTH, bs, WINDOW, B_KW), wts,
        batch=bs, length=ls, tm=bs * ls, gdn_tb=128, gdn_c=128, swa_tb=ls, swa_c=ls, mask_start=False)
    return (y_p, y_s, conv_p, delta_p, k_p, v_p, conv_s, delta_s, k_s, v_s)
```

```python
import functools

import numpy as np
import jax
import jax.numpy as jnp
from jax import lax
from jax.experimental import pallas as pl
from jax.experimental.pallas import tpu as pltpu

F32 = jnp.float32
BF16 = jnp.bfloat16

D_MODEL = 1024
DEPTH = 4
PAST_LEN = 4096
EPS = 1e-6
A_HEADS = 4
A_DK = 128
A_DV = 128
A_QK = A_HEADS * A_DK
A_VW = A_HEADS * A_DV
A_QKV = 2 * A_QK + A_VW
CONV_W = 4
B_HEADS = 8
B_KV_HEADS = 2
B_HD = 64
B_QW = B_HEADS * B_HD
B_KW = B_KV_HEADS * B_HD
WINDOW = 128
ROPE_DIM = B_HD // 4
ROPE_THETA = 500000.0
D_MIX = A_VW + B_QW
D_FF = 2816

LANES = 128
SUBLANES = 8
GATE_W = LANES
O_QKV = 0
O_Z = O_QKV + A_QKV
O_B = O_Z + A_VW
O_A = O_B + GATE_W
O_QB = O_A + GATE_W
O_KB = O_QB + B_QW
O_VB = O_KB + B_KW
D_IN_PACKED = O_VB + B_KW

VMEM_LIMIT = 56 * 1024 * 1024


def _params(sem, vmem=VMEM_LIMIT):
    return pltpu.CompilerParams(dimension_semantics=sem, vmem_limit_bytes=vmem)


def _resident(shape):
    nd = len(shape)
    return pl.BlockSpec(shape, lambda *_: (0,) * nd, pipeline_mode=pl.Buffered(1))


def _rms(x, w):
    return x * lax.rsqrt(jnp.mean(x * x, axis=-1, keepdims=True) + EPS) * w


def _silu(x):
    return x * jax.nn.sigmoid(x)


def _mm(a, b):
    return jnp.dot(a.astype(BF16), b.astype(BF16), preferred_element_type=F32)


def _ffn_body(x_ref, nw_ref, win_ref, wout_ref, o_ref, *, fc):
    x = x_ref[...]
    h = _rms(x, nw_ref[...]).astype(BF16)
    acc = None
    for c in range(0, D_FF, fc):
        gate = jnp.dot(h, win_ref[:, c:c + fc], preferred_element_type=F32)
        up = jnp.dot(h, win_ref[:, D_FF + c:D_FF + c + fc], preferred_element_type=F32)
        act = (_silu(gate) * up).astype(BF16)
        part = jnp.dot(act, wout_ref[c:c + fc, :], preferred_element_type=F32)
        acc = part if acc is None else acc + part
    o_ref[...] = x + 0.5 * acc


def _ffn(x, nw, w_in, w_out, *, tm, fc=1408):
    m = x.shape[0]
    return pl.pallas_call(
        functools.partial(_ffn_body, fc=fc),
        grid=(m // tm,),
        in_specs=[pl.BlockSpec((tm, D_MODEL), lambda i: (i, 0)),
                  _resident((1, D_MODEL)),
                  _resident((D_MODEL, 2 * D_FF)),
                  _resident((D_FF, D_MODEL))],
        out_specs=pl.BlockSpec((tm, D_MODEL), lambda i: (i, 0)),
        out_shape=jax.ShapeDtypeStruct((m, D_MODEL), F32),
        compiler_params=_params(("parallel",)),
        name="ffn",
    )(x, nw, w_in, w_out)


_IN_GROUPS = ((O_QKV, A_QKV), (O_Z, A_VW), (O_B, GATE_W), (O_A, GATE_W),
              (O_QB, B_QW), (O_KB, B_KW), (O_VB, B_KW))


def _proj_in_body(x_ref, nw_ref, w_ref, *out_refs):
    h = _rms(x_ref[...], nw_ref[...]).astype(BF16)
    for (off, width), o_ref in zip(_IN_GROUPS, out_refs):
        o_ref[...] = jnp.dot(h, w_ref[:, off:off + width], preferred_element_type=F32)


def _proj_in(x, nw, w, *, tm):
    m = x.shape[0]
    return pl.pallas_call(
        _proj_in_body,
        grid=(m // tm,),
        in_specs=[pl.BlockSpec((tm, D_MODEL), lambda i: (i, 0)),
                  _resident((1, D_MODEL)),
                  _resident((D_MODEL, D_IN_PACKED))],
        out_specs=[pl.BlockSpec((tm, width), lambda i: (i, 0)) for _, width in _IN_GROUPS],
        out_shape=[jax.ShapeDtypeStruct((m, width), F32) for _, width in _IN_GROUPS],
        compiler_params=_params(("parallel",)),
        name="proj_in",
    )(x, nw, w)


def _proj_out_body(x_ref, oa_ref, ob_ref, w_ref, o_ref):
    o_ref[...] = (x_ref[...]
                  + jnp.dot(oa_ref[...].astype(BF16), w_ref[0:A_VW, :], preferred_element_type=F32)
                  + jnp.dot(ob_ref[...].astype(BF16), w_ref[A_VW:D_MIX, :], preferred_element_type=F32))


def _proj_out(x, oa, ob, w, *, tm):
    m = x.shape[0]
    return pl.pallas_call(
        _proj_out_body,
        grid=(m // tm,),
        in_specs=[pl.BlockSpec((tm, D_MODEL), lambda i: (i, 0)),
                  pl.BlockSpec((tm, A_VW), lambda i: (i, 0)),
                  pl.BlockSpec((tm, B_QW), lambda i: (i, 0)),
                  _resident((D_MIX, D_MODEL))],
        out_specs=pl.BlockSpec((tm, D_MODEL), lambda i: (i, 0)),
        out_shape=jax.ShapeDtypeStruct((m, D_MODEL), F32),
        compiler_params=_params(("parallel",)),
        name="proj_out",
    )(x, oa, ob, w)


def _norm_body(x_ref, nw_ref, o_ref):
    o_ref[...] = _rms(x_ref[...], nw_ref[...])


def _final_norm(x, nw, *, tm):
    m = x.shape[0]
    return pl.pallas_call(
        _norm_body,
        grid=(m // tm,),
        in_specs=[pl.BlockSpec((tm, D_MODEL), lambda i: (i, 0)), _resident((1, D_MODEL))],
        out_specs=pl.BlockSpec((tm, D_MODEL), lambda i: (i, 0)),
        out_shape=jax.ShapeDtypeStruct((m, D_MODEL), F32),
        compiler_params=_params(("parallel",)),
        name="final_norm",
    )(x, nw)


def _chunk_cumsum(g, c):
    row = lax.broadcasted_iota(jnp.int32, g.shape, 0)
    s = 1
    while s < c:
        g = g + jnp.where(row >= s, pltpu.roll(g, s, 0), 0.0)
        s *= 2
    return g


INV_BLOCK = 32


def _unit_lower_inverses_minus_eye(ls, c):
    bs = min(INV_BLOCK, c)
    row = lax.broadcasted_iota(jnp.int32, (c, c), 0)
    col = lax.broadcasted_iota(jnp.int32, (c, c), 1)
    same = lambda b: (row // b) == (col // b)
    diag = [jnp.where(same(bs), l, 0.0) for l in ls] if bs < c else ls
    p = [-d for d in diag]
    m = [_mm(d, d) for d in diag]
    for _ in range(int(np.log2(bs)) - 2):
        pm = [_mm(pi, mi) for pi, mi in zip(p, m)]
        p = [pi + mi + pmi for pi, mi, pmi in zip(p, m, pm)]
        m = [_mm(mi, mi) for mi in m]
    pm = [_mm(pi, mi) for pi, mi in zip(p, m)]
    p = [pi + mi + pmi for pi, mi, pmi in zip(p, m, pm)]
    b = bs
    while b < c:
        sel = same(2 * b) & jnp.logical_not(same(b))
        off = [jnp.where(sel, l, 0.0) for l in ls]
        x = [oi + _mm(pi, oi) for pi, oi in zip(p, off)]
        p = [pi - (xi + _mm(xi, pi)) for pi, xi in zip(p, x)]
        b *= 2
    return p


def _gdn_body(qkv_ref, z_ref, bl_ref, al_ref, cbuf_ref, s0_ref, cw_ref, alog_ref, dtb_ref, gw_ref,
              o_ref, nbuf_ref, snew_ref, xp_ref, act_ref, s_ref, u_ref, wq_ref, qk_ref, kdec_ref,
              *, tb, c, n_valid):
    t = pl.program_id(1)
    halo = CONV_W - 1
    base = SUBLANES

    @pl.when(t == 0)
    def _():
        xp_ref[base - halo:base, :] = cbuf_ref[...]
        s_ref[...] = s0_ref[...]

    xp_ref[base:base + tb, :] = qkv_ref[...]
    cw = cw_ref[...]
    conv = xp_ref[base - halo:base - halo + tb, :] * cw[0:1, :]
    for j in range(1, CONV_W):
        conv = conv + xp_ref[base - halo + j:base - halo + j + tb, :] * cw[j:j + 1, :]
    act_ref[...] = _silu(conv)

    @pl.when(t == pl.num_programs(1) - 1)
    def _():
        nbuf_ref[...] = xp_ref[base + n_valid - halo:base + n_valid, :]

    xp_ref[base - halo:base, :] = xp_ref[base + tb - halo:base + tb, :]

    beta_all = jax.nn.sigmoid(bl_ref[...])
    a_in = al_ref[...] + dtb_ref[...]
    softplus = jnp.maximum(a_in, 0.0) + jnp.log1p(jnp.exp(-jnp.abs(a_in)))
    g_all = -jnp.exp(alog_ref[...]) * softplus
    if n_valid < tb:
        live = lax.broadcasted_iota(jnp.int32, (tb, GATE_W), 0) < n_valid
        beta_all = jnp.where(live, beta_all, 0.0)
        g_all = jnp.where(live, g_all, 0.0)

    row = lax.broadcasted_iota(jnp.int32, (c, c), 0)
    col = lax.broadcasted_iota(jnp.int32, (c, c), 1)
    causal = row >= col
    strict = row > col
    gw = gw_ref[...]
    n_chunks = tb // c

    g_tots = []
    lmats, rhss = [], []
    for ci in range(n_chunks):
        r0 = ci * c
        beta = beta_all[r0:r0 + c, :]
        g_cum = _chunk_cumsum(g_all[r0:r0 + c, :], c)
        g_cum_t = g_cum.T
        g_last = g_cum[c - 1:c, :]
        e_g = jnp.exp(g_cum)
        e_gl = jnp.exp(g_last - g_cum)
        g_tots.append(jnp.exp(g_last))
        for h in range(A_HEADS):
            i = ci * A_HEADS + h
            q = act_ref[r0:r0 + c, h * A_DK:(h + 1) * A_DK]
            k = act_ref[r0:r0 + c, A_QK + h * A_DK:A_QK + (h + 1) * A_DK]
            v = act_ref[r0:r0 + c, 2 * A_QK + h * A_DV:2 * A_QK + (h + 1) * A_DV]
            q = q * lax.rsqrt(jnp.sum(q * q, axis=-1, keepdims=True) + EPS) * (A_DK ** -0.5)
            k = k * lax.rsqrt(jnp.sum(k * k, axis=-1, keepdims=True) + EPS)
            b_col = beta[:, h:h + 1]
            eg_col = e_g[:, h:h + 1]
            decay = jnp.exp(jnp.where(causal, g_cum[:, h:h + 1] - g_cum_t[h:h + 1, :], -jnp.inf))
            k16 = k.astype(BF16)
            qk_kk = lax.dot_general(jnp.concatenate([q.astype(BF16), k16], axis=0), k16,
                                    (((1,), (1,)), ((), ())), preferred_element_type=F32)
            qk_ref[i] = (qk_kk[0:c, :] * decay).astype(BF16)
            lmats.append(jnp.where(strict, qk_kk[c:2 * c, :] * decay * b_col, 0.0))
            rhss.append(jnp.concatenate([v * b_col, k * (b_col * eg_col)], axis=1))
            wq_ref[i, c:2 * c, :] = (q * eg_col).astype(BF16)
            kdec_ref[i] = (k * e_gl[:, h:h + 1]).astype(BF16)
    tinvs = _unit_lower_inverses_minus_eye(lmats, c)
    for i, (tinv, rhs) in enumerate(zip(tinvs, rhss)):
        sol = rhs + _mm(tinv, rhs)
        u_ref[i] = sol[:, 0:A_DV]
        wq_ref[i, 0:c, :] = sol[:, A_DV:A_DV + A_DK].astype(BF16)

    for ci in range(n_chunks):
        r0 = ci * c
        idx = [ci * A_HEADS + h for h in range(A_HEADS)]
        s_old = [s_ref[h] for h in range(A_HEADS)]
        wq_s = [jnp.dot(wq_ref[i], s.astype(BF16), preferred_element_type=F32) for i, s in zip(idx, s_old)]
        v_new = [u_ref[i] - x[0:c, :] for i, x in zip(idx, wq_s)]
        v16 = [x.astype(BF16) for x in v_new]
        for h in range(A_HEADS):
            s_ref[h] = s_old[h] * g_tots[ci][:, h:h + 1] + lax.dot_general(
                kdec_ref[idx[h]], v16[h], (((0,), (0,)), ((), ())), preferred_element_type=F32)
        for h in range(A_HEADS):
            o = wq_s[h][c:2 * c, :] + jnp.dot(qk_ref[idx[h]], v16[h], preferred_element_type=F32)
            o = o * lax.rsqrt(jnp.mean(o * o, axis=-1, keepdims=True) + EPS) * gw
            o_ref[r0:r0 + c, h * A_DV:(h + 1) * A_DV] = o * _silu(z_ref[r0:r0 + c, h * A_DV:(h + 1) * A_DV])

    @pl.when(t == pl.num_programs(1) - 1)
    def _():
        snew_ref[...] = s_ref[...]


def _gdn(qkv, z, bl, al, cbuf, s0, cw, alog, dtb, gw, *, tb, c, n_valid):
    b, l, _ = qkv.shape
    nt = l // tb
    assert n_valid == tb or nt == 1
    n_items = (tb // c) * A_HEADS
    blk = lambda width: pl.BlockSpec((None, tb, width), lambda i, j: (i, j, 0))
    return pl.pallas_call(
        functools.partial(_gdn_body, tb=tb, c=c, n_valid=n_valid),
        grid=(b, nt),
        in_specs=[blk(A_QKV), blk(A_VW), blk(GATE_W), blk(GATE_W),
                  pl.BlockSpec((None, CONV_W - 1, A_QKV), lambda i, j: (i, 0, 0)),
                  pl.BlockSpec((None, A_HEADS, A_DK, A_DV), lambda i, j: (i, 0, 0, 0)),
                  pl.BlockSpec((CONV_W, A_QKV), lambda i, j: (0, 0)),
                  pl.BlockSpec((1, GATE_W), lambda i, j: (0, 0)),
                  pl.BlockSpec((1, GATE_W), lambda i, j: (0, 0)),
                  pl.BlockSpec((1, A_DV), lambda i, j: (0, 0))],
        out_specs=[blk(A_VW),
                   pl.BlockSpec((None, CONV_W - 1, A_QKV), lambda i, j: (i, 0, 0)),
                   pl.BlockSpec((None, A_HEADS, A_DK, A_DV), lambda i, j: (i, 0, 0, 0))],
        out_shape=[jax.ShapeDtypeStruct((b, l, A_VW), F32),
                   jax.ShapeDtypeStruct((b, CONV_W - 1, A_QKV), F32),
                   jax.ShapeDtypeStruct((b, A_HEADS, A_DK, A_DV), F32)],
        scratch_shapes=[pltpu.VMEM((SUBLANES + tb, A_QKV), F32),
                        pltpu.VMEM((tb, A_QKV), F32),
                        pltpu.VMEM((A_HEADS, A_DK, A_DV), F32),
                        pltpu.VMEM((n_items, c, A_DV), F32),
                        pltpu.VMEM((n_items, 2 * c, A_DK), BF16),
                        pltpu.VMEM((n_items, c, c), BF16),
                        pltpu.VMEM((n_items, c, A_DK), BF16)],
        compiler_params=_params(("parallel", "arbitrary")),
        name="gdn",
    )(qkv, z, bl, al, cbuf, s0, cw, alog, dtb, gw)


def _rope_table_body(inv_ref, ma_ref, mb_ref, cos_ref, sa_ref, sb_ref, *, tm, pos0):
    pos = pos0 + pl.program_id(0) * tm + lax.broadcasted_iota(jnp.int32, (tm, LANES), 0)
    ang = pos.astype(F32) * inv_ref[...]
    sin = jnp.sin(ang)
    cos_ref[...] = jnp.cos(ang)
    sa_ref[...] = sin * ma_ref[...]
    sb_ref[...] = -sin * mb_ref[...]


def _rope_tables(inv_row, ma, mb, *, length, pos0, tm):
    row = pl.BlockSpec((1, LANES), lambda i: (0, 0))
    out = pl.BlockSpec((tm, LANES), lambda i: (i, 0))
    return pl.pallas_call(
        functools.partial(_rope_table_body, tm=tm, pos0=pos0),
        grid=(length // tm,),
        in_specs=[row, row, row],
        out_specs=[out, out, out],
        out_shape=[jax.ShapeDtypeStruct((length, LANES), F32)] * 3,
        compiler_params=_params(("parallel",)),
        name="rope_tables",
    )(inv_row, ma, mb)


SWA_LOCKSTEP = 4

def _swa_body(qb_ref, kb_ref, vb_ref, cos_ref, sa_ref, sb_ref, kc_ref, vc_ref, sinks_ref,
              o_ref, newk_ref, newv_ref, kx_ref, vx_ref, qs_ref, kvar_ref, vvar_ref,
              *, tb, c, mask_start):
    t = pl.program_id(1)
    half = B_HD
    rot = ROPE_DIM // 2
    w = WINDOW + c

    @pl.when(t == 0)
    def _():
        kx_ref[0:WINDOW, :] = kc_ref[...]
        vx_ref[0:WINDOW, :] = vc_ref[...]

    cos = cos_ref[...]
    sa = sa_ref[...]
    sb = sb_ref[...]

    def rope(x):
        return x * cos + pltpu.roll(x, rot, 1) * sa + pltpu.roll(x, LANES - rot, 1) * sb

    kx_ref[WINDOW:WINDOW + tb, :] = rope(kb_ref[...])
    vx_ref[WINDOW:WINDOW + tb, :] = vb_ref[...]
    for s in range(B_QW // LANES):
        qs_ref[:, s * LANES:(s + 1) * LANES] = (
            rope(qb_ref[:, s * LANES:(s + 1) * LANES]) * (B_HD ** -0.5)).astype(BF16)

    lo = lax.broadcasted_iota(jnp.int32, (WINDOW + tb, LANES), 1) < half
    for src, dst in ((kx_ref, kvar_ref), (vx_ref, vvar_ref)):
        full = src[...]
        swapped = pltpu.roll(full, half, 1)
        dst[0] = jnp.where(lo, full, 0.0).astype(BF16)
        dst[1] = jnp.where(lo, 0.0, swapped).astype(BF16)
        dst[2] = jnp.where(lo, swapped, 0.0).astype(BF16)
        dst[3] = jnp.where(lo, 0.0, full).astype(BF16)

    first_rows = lax.broadcasted_iota(jnp.int32, (2 * c, 1), 0) < c
    kcol = lax.broadcasted_iota(jnp.int32, (2 * c, w), 1)

    sks = [jnp.where(first_rows, sinks_ref[4 * g + hh], sinks_ref[4 * g + 2 + hh])
           for g in range(B_KV_HEADS) for hh in range(2)]
    n_chunks = tb // c
    for i0 in range(0, n_chunks, SWA_LOCKSTEP):
        items = [(i, g, hh) for i in range(i0, min(i0 + SWA_LOCKSTEP, n_chunks))
                 for g in range(B_KV_HEADS) for hh in range(2)]
        lhs = {(i, g): jnp.concatenate([qs_ref[i * c:(i + 1) * c, (2 * g) * LANES:(2 * g + 1) * LANES],
                                        qs_ref[i * c:(i + 1) * c, (2 * g + 1) * LANES:(2 * g + 2) * LANES]], axis=0)
               for i, g, hh in items if hh == 0}
        s = [lax.dot_general(lhs[i, g], kvar_ref[2 * g + hh, i * c:i * c + w, :],
                             (((1,), (1,)), ((), ())), preferred_element_type=F32) for i, g, hh in items]
        if mask_start:
            s = [jnp.where(t * tb + i * c - WINDOW + kcol >= 0, si, -jnp.inf) for si, (i, g, hh) in zip(s, items)]
        m = [jnp.maximum(jnp.max(si, axis=-1, keepdims=True), sks[2 * g + hh]) for si, (i, g, hh) in zip(s, items)]
        p = [jnp.exp(si - mi) for si, mi in zip(s, m)]
        den = [jnp.sum(pi, axis=-1, keepdims=True) + jnp.exp(sks[2 * g + hh] - mi)
               for pi, mi, (i, g, hh) in zip(p, m, items)]
        pv = [jnp.dot(pi.astype(BF16), vvar_ref[2 * g + hh, i * c:i * c + w, :], preferred_element_type=F32) / di
              for pi, di, (i, g, hh) in zip(p, den, items)]
        for j in range(0, len(items), 2):
            i, g, _ = items[j]
            o = pv[j] + pv[j + 1]
            o_ref[i * c:(i + 1) * c, (2 * g) * LANES:(2 * g + 1) * LANES] = o[0:c, :]
            o_ref[i * c:(i + 1) * c, (2 * g + 1) * LANES:(2 * g + 2) * LANES] = o[c:2 * c, :]

    @pl.when(t == pl.num_programs(1) - 1)
    def _():
        newk_ref[...] = kx_ref[tb:tb + WINDOW, :]
        newv_ref[...] = vx_ref[tb:tb + WINDOW, :]

    if tb >= WINDOW:
        kx_ref[0:WINDOW, :] = kx_ref[tb:tb + WINDOW, :]
        vx_ref[0:WINDOW, :] = vx_ref[tb:tb + WINDOW, :]


def _swa(qb, kb, vb, cos, sa, sb, kc, vc, sinks, *, tb, c, mask_start):
    b, l, _ = qb.shape
    nt = l // tb
    assert tb >= WINDOW or nt == 1
    blk = lambda width: pl.BlockSpec((None, tb, width), lambda i, j: (i, j, 0))
    tab = pl.BlockSpec((tb, LANES), lambda i, j: (j, 0))
    cache = pl.BlockSpec((None, WINDOW, B_KW), lambda i, j: (i, 0, 0))
    return pl.pallas_call(
        functools.partial(_swa_body, tb=tb, c=c, mask_start=mask_start),
        grid=(b, nt),
        in_specs=[blk(B_QW), blk(B_KW), blk(B_KW), tab, tab, tab, cache, cache,
                  pl.BlockSpec(memory_space=pltpu.SMEM)],
        out_specs=[blk(B_QW), cache, cache],
        out_shape=[jax.ShapeDtypeStruct((b, l, B_QW), F32),
                   jax.ShapeDtypeStruct((b, WINDOW, B_KW), F32),
                   jax.ShapeDtypeStruct((b, WINDOW, B_KW), F32)],
        scratch_shapes=[pltpu.VMEM((WINDOW + tb, B_KW), F32),
                        pltpu.VMEM((WINDOW + tb, B_KW), F32),
                        pltpu.VMEM((tb, B_QW), BF16),
                        pltpu.VMEM((4, WINDOW + tb, LANES), BF16),
                        pltpu.VMEM((4, WINDOW + tb, LANES), BF16)],
        compiler_params=_params(("parallel", "arbitrary")),
        name="swa",
    )(qb, kb, vb, cos, sa, sb, kc, vc, sinks)


def _pack_w_in(w):
    offs = np.cumsum([0, A_QKV, A_VW, A_HEADS, A_HEADS, B_QW, B_KW, B_KW]).tolist()
    qkv, z, bg, ag, qb, kb, vb = [w[..., offs[i]:offs[i + 1]] for i in range(7)]
    pad = lambda g: jnp.pad(g, ((0, 0), (0, 0), (0, GATE_W - g.shape[-1])))
    return jnp.concatenate([qkv, z, pad(bg), pad(ag), qb, kb, vb], axis=-1).astype(BF16)


def _pad_lanes(v):
    return jnp.pad(v, ((0, 0), (0, GATE_W - v.shape[-1])))[:, None, :]


def _trunk(x, rope_tabs, conv_bufs, s0s, k_caches, v_caches, wts, *, batch, length, tm,
           gdn_tb, gdn_c, swa_tb, swa_c, mask_start):
    (norm_ff1, ff1_in, ff1_out, norm_mix, w_in, conv_w, alog, dtb, gnorm, sinks, w_out,
     norm_ff2, ff2_in, ff2_out, norm_final) = wts
    cos, sa, sb = rope_tabs
    lpad = -(-length // gdn_tb) * gdn_tb
    bufs, states, ks, vs = [], [], [], []

    def seq(a, width):
        return a.reshape(batch, length, width)

    def seq_pad(a, width):
        a = seq(a, width)
        return a if lpad == length else jnp.pad(a, ((0, 0), (0, lpad - length), (0, 0)))

    for l in range(DEPTH):
        x = _ffn(x, norm_ff1[l], ff1_in[l], ff1_out[l], tm=tm)
        qkv, z, bg, ag, qb, kb, vb = _proj_in(x, norm_mix[l], w_in[l], tm=tm)
        o_a, nbuf, s_new = _gdn(seq_pad(qkv, A_QKV), seq_pad(z, A_VW), seq_pad(bg, GATE_W), seq_pad(ag, GATE_W),
                                conv_bufs[l], s0s[l], conv_w[l], alog[l], dtb[l], gnorm[l],
                                tb=gdn_tb, c=gdn_c, n_valid=min(length, gdn_tb))
        o_b, nk, nv = _swa(seq(qb, B_QW), seq(kb, B_KW), seq(vb, B_KW), cos, sa, sb,
                           k_caches[l], v_caches[l], sinks[l], tb=swa_tb, c=swa_c, mask_start=mask_start)
        o_a = o_a[:, :length].reshape(batch * length, A_VW)
        x = _proj_out(x, o_a, o_b.reshape(batch * length, B_QW), w_out[l], tm=tm)
        x = _ffn(x, norm_ff2[l], ff2_in[l], ff2_out[l], tm=tm)
        bufs.append(nbuf)
        states.append(s_new)
        ks.append(nk.reshape(batch, WINDOW, B_KV_HEADS, B_HD))
        vs.append(nv.reshape(batch, WINDOW, B_KV_HEADS, B_HD))
    y = _final_norm(x, norm_final, tm=tm).reshape(batch, length, D_MODEL)
    return y, jnp.stack(bufs), jnp.stack(states), jnp.stack(ks), jnp.stack(vs)


def kernel(x_prompt, x_sample, cache_conv, state_delta, cache_k, cache_v, norm_ff1, ff1_w_in, ff1_w_out, norm_mix, w_mix_in, conv_w, a_log, dt_bias, gnorm_w, sinks, w_mix_out, norm_ff2, ff2_w_in, ff2_w_out, norm_final):
    bp, lp, _ = x_prompt.shape
    bs, ls, _ = x_sample.shape
    rows = cache_k.shape[2]
    assert rows == WINDOW

    wts = (norm_ff1[:, None, :], ff1_w_in.astype(BF16), ff1_w_out.astype(BF16), norm_mix[:, None, :],
           _pack_w_in(w_mix_in), conv_w, _pad_lanes(a_log), _pad_lanes(dt_bias), gnorm_w[:, None, :], sinks,
           w_mix_out.astype(BF16), norm_ff2[:, None, :], ff2_w_in.astype(BF16), ff2_w_out.astype(BF16),
           norm_final[None, :])

    inv = jnp.power(ROPE_THETA, -jnp.arange(0, ROPE_DIM, 2, dtype=F32) / ROPE_DIM)
    rot = ROPE_DIM // 2
    head_row = jnp.concatenate([inv, inv, jnp.zeros((B_HD - ROPE_DIM,), F32)])
    inv_row = jnp.tile(head_row, LANES // B_HD)[None, :]
    d = np.arange(LANES) % B_HD
    ma = jnp.asarray(((d >= rot) & (d < ROPE_DIM)).astype(np.float32))[None, :]
    mb = jnp.asarray((d < rot).astype(np.float32))[None, :]

    zero_buf = jnp.zeros((DEPTH, bp, CONV_W - 1, A_QKV), F32)
    zero_state = jnp.zeros((DEPTH, bp, A_HEADS, A_DK, A_DV), F32)
    zero_kv = jnp.zeros((DEPTH, bp, WINDOW, B_KW), F32)
    tabs_p = _rope_tables(inv_row, ma, mb, length=lp, pos0=0, tm=512)
    y_p, conv_p, delta_p, k_p, v_p = _trunk(
        x_prompt.reshape(bp * lp, D_MODEL), tabs_p, zero_buf, zero_state, zero_kv, zero_kv, wts,
        batch=bp, length=lp, tm=512, gdn_tb=512, gdn_c=128, swa_tb=512, swa_c=64, mask_start=True)

    tabs_s = _rope_tables(inv_row, ma, mb, length=ls, pos0=PAST_LEN, tm=ls)
    y_s, conv_s, delta_s, k_s, v_s = _trunk(
        x_sample.reshape(bs * ls, D_MODEL), tabs_s, cache_conv, state_delta,
        cache_k.reshape(DEPTH, bs, WINDOW, B_KW), cache_v.reshape(DEPTH, bs, WINDOW, B_KW), wts,
        batch=bs, length=ls, tm=bs * ls, gdn_tb=128, gdn_c=128, swa_tb=ls, swa_c=ls, mask_start=False)
    return (y_p, y_s, conv_p, delta_p, k_p, v_p, conv_s, delta_s, k_s, v_s)
```

```python
import functools

import numpy as np
import jax
import jax.numpy as jnp
from jax import lax
from jax.experimental import pallas as pl
from jax.experimental.pallas import tpu as pltpu

F32 = jnp.float32
BF16 = jnp.bfloat16

D_MODEL = 1024
DEPTH = 4
PAST_LEN = 4096
EPS = 1e-6
A_HEADS = 4
A_DK = 128
A_DV = 128
A_QK = A_HEADS * A_DK
A_VW = A_HEADS * A_DV
A_QKV = 2 * A_QK + A_VW
CONV_W = 4
B_HEADS = 8
B_KV_HEADS = 2
B_HD = 64
B_QW = B_HEADS * B_HD
B_KW = B_KV_HEADS * B_HD
WINDOW = 128
ROPE_DIM = B_HD // 4
ROPE_THETA = 500000.0
D_MIX = A_VW + B_QW
D_FF = 2816

LANES = 128
SUBLANES = 8
GATE_W = LANES
O_QKV = 0
O_Z = O_QKV + A_QKV
O_B = O_Z + A_VW
O_A = O_B + GATE_W
O_QB = O_A + GATE_W
O_KB = O_QB + B_QW
O_VB = O_KB + B_KW
D_IN_PACKED = O_VB + B_KW

VMEM_LIMIT = 56 * 1024 * 1024


def _params(sem, vmem=VMEM_LIMIT):
    return pltpu.CompilerParams(dimension_semantics=sem, vmem_limit_bytes=vmem)


def _resident(shape):
    nd = len(shape)
    return pl.BlockSpec(shape, lambda *_: (0,) * nd, pipeline_mode=pl.Buffered(1))


def _rms(x, w):
    return x * lax.rsqrt(jnp.mean(x * x, axis=-1, keepdims=True) + EPS) * w


def _silu(x):
    return x * jax.nn.sigmoid(x)


def _mm(a, b):
    return jnp.dot(a.astype(BF16), b.astype(BF16), preferred_element_type=F32)


def _swiglu_residual(x, nw_ref, win_ref, wout_ref, fc):
    h = _rms(x, nw_ref[...]).astype(BF16)
    acc = None
    for c in range(0, D_FF, fc):
        gate = jnp.dot(h, win_ref[:, c:c + fc], preferred_element_type=F32)
        up = jnp.dot(h, win_ref[:, D_FF + c:D_FF + c + fc], preferred_element_type=F32)
        act = (_silu(gate) * up).astype(BF16)
        part = jnp.dot(act, wout_ref[c:c + fc, :], preferred_element_type=F32)
        acc = part if acc is None else acc + part
    return x + 0.5 * acc


def _ffn_body(x_ref, nw_ref, win_ref, wout_ref, o_ref, *, fc):
    o_ref[...] = _swiglu_residual(x_ref[...], nw_ref, win_ref, wout_ref, fc)


def _ffn(x, nw, w_in, w_out, *, tm, fc=1408):
    m = x.shape[0]
    return pl.pallas_call(
        functools.partial(_ffn_body, fc=fc),
        grid=(m // tm,),
        in_specs=[pl.BlockSpec((tm, D_MODEL), lambda i: (i, 0)),
                  _resident((1, D_MODEL)),
                  _resident((D_MODEL, 2 * D_FF)),
                  _resident((D_FF, D_MODEL))],
        out_specs=pl.BlockSpec((tm, D_MODEL), lambda i: (i, 0)),
        out_shape=jax.ShapeDtypeStruct((m, D_MODEL), F32),
        compiler_params=_params(("parallel",)),
        name="ffn",
    )(x, nw, w_in, w_out)


def _mix_out_ffn_body(x_ref, oa_ref, ob_ref, wo_ref, nw_ref, win_ref, wout_ref, *rest, fc, final):
    x = (x_ref[...]
         + jnp.dot(oa_ref[...].astype(BF16), wo_ref[0:A_VW, :], preferred_element_type=F32)
         + jnp.dot(ob_ref[...].astype(BF16), wo_ref[A_VW:D_MIX, :], preferred_element_type=F32))
    y = _swiglu_residual(x, nw_ref, win_ref, wout_ref, fc)
    if final:
        nf_ref, o_ref = rest
        o_ref[...] = _rms(y, nf_ref[...])
    else:
        (o_ref,) = rest
        o_ref[...] = y


def _mix_out_ffn(x, oa, ob, wo, nw, w_in, w_out, norm_final=None, *, tm, fc=1408):
    m = x.shape[0]
    final = norm_final is not None
    row = lambda width: pl.BlockSpec((tm, width), lambda i: (i, 0))
    in_specs = [row(D_MODEL), row(A_VW), row(B_QW), _resident((D_MIX, D_MODEL)), _resident((1, D_MODEL)),
                _resident((D_MODEL, 2 * D_FF)), _resident((D_FF, D_MODEL))]
    args = [x, oa, ob, wo, nw, w_in, w_out]
    if final:
        in_specs.append(_resident((1, D_MODEL)))
        args.append(norm_final)
    return pl.pallas_call(
        functools.partial(_mix_out_ffn_body, fc=fc, final=final),
        grid=(m // tm,),
        in_specs=in_specs,
        out_specs=row(D_MODEL),
        out_shape=jax.ShapeDtypeStruct((m, D_MODEL), F32),
        compiler_params=_params(("parallel",)),
        name="mix_out_ffn",
    )(*args)


_IN_GROUPS = ((O_QKV, A_QKV), (O_Z, A_VW), (O_B, GATE_W), (O_A, GATE_W),
              (O_QB, B_QW), (O_KB, B_KW), (O_VB, B_KW))
HALO = CONV_W - 1
HALO_BASE = SUBLANES


def _proj_in_body(x_ref, nw_ref, w_ref, cbuf_ref, cw_ref, act_ref, *rest, tm):
    out_refs, (nbuf_ref, xp_ref) = rest[:-2], rest[-2:]
    j = pl.program_id(1)

    @pl.when(j == 0)
    def _():
        xp_ref[HALO_BASE - HALO:HALO_BASE, :] = cbuf_ref[...]

    h = _rms(x_ref[...], nw_ref[...]).astype(BF16)
    xp_ref[HALO_BASE:HALO_BASE + tm, :] = jnp.dot(h, w_ref[:, O_QKV:O_QKV + A_QKV], preferred_element_type=F32)
    for (off, width), o_ref in zip(_IN_GROUPS[1:], out_refs):
        o_ref[...] = jnp.dot(h, w_ref[:, off:off + width], preferred_element_type=F32)
    cw = cw_ref[...]
    lo = HALO_BASE - HALO
    conv = xp_ref[lo:lo + tm, :] * cw[0:1, :]
    for t in range(1, CONV_W):
        conv = conv + xp_ref[lo + t:lo + t + tm, :] * cw[t:t + 1, :]
    act_ref[...] = _silu(conv)

    @pl.when(j == pl.num_programs(1) - 1)
    def _():
        nbuf_ref[...] = xp_ref[HALO_BASE + tm - HALO:HALO_BASE + tm, :]

    xp_ref[HALO_BASE - HALO:HALO_BASE, :] = xp_ref[HALO_BASE + tm - HALO:HALO_BASE + tm, :]


def _proj_in(x, nw, w, cbuf, cw, *, tm):
    b, l, _ = x.shape
    blk = lambda width: pl.BlockSpec((None, tm, width), lambda i, j: (i, j, 0))
    buf = pl.BlockSpec((None, HALO, A_QKV), lambda i, j: (i, 0, 0))
    widths = [width for _, width in _IN_GROUPS]
    return pl.pallas_call(
        functools.partial(_proj_in_body, tm=tm),
        grid=(b, l // tm),
        in_specs=[blk(D_MODEL), _resident((1, D_MODEL)), _resident((D_MODEL, D_IN_PACKED)), buf,
                  pl.BlockSpec((CONV_W, A_QKV), lambda i, j: (0, 0))],
        out_specs=[blk(width) for width in widths] + [buf],
        out_shape=[jax.ShapeDtypeStruct((b, l, width), F32) for width in widths]
                  + [jax.ShapeDtypeStruct((b, HALO, A_QKV), F32)],
        scratch_shapes=[pltpu.VMEM((HALO_BASE + tm, A_QKV), F32)],
        compiler_params=_params(("parallel", "arbitrary")),
        name="proj_in",
    )(x, nw, w, cbuf, cw)


def _chunk_cumsum(g, tril):
    g1 = g.astype(BF16)
    r1 = g - g1.astype(F32)
    g2 = r1.astype(BF16)
    g3 = (r1 - g2.astype(F32)).astype(BF16)
    parts = jnp.dot(tril, jnp.concatenate([g1, g2, g3], axis=1), preferred_element_type=F32)
    return parts[:, 0:LANES] + parts[:, LANES:2 * LANES] + parts[:, 2 * LANES:3 * LANES]


INV_BLOCK = 32


def _unit_lower_inverses_minus_eye(ls, c):
    bs = min(INV_BLOCK, c)
    row = lax.broadcasted_iota(jnp.int32, (c, c), 0)
    col = lax.broadcasted_iota(jnp.int32, (c, c), 1)
    same = lambda b: (row // b) == (col // b)
    diag = [jnp.where(same(bs), l, 0.0) for l in ls] if bs < c else ls
    p = [-d for d in diag]
    m = [_mm(d, d) for d in diag]
    for _ in range(int(np.log2(bs)) - 2):
        pm = [_mm(pi, mi) for pi, mi in zip(p, m)]
        p = [pi + mi + pmi for pi, mi, pmi in zip(p, m, pm)]
        m = [_mm(mi, mi) for mi in m]
    pm = [_mm(pi, mi) for pi, mi in zip(p, m)]
    p = [pi + mi + pmi for pi, mi, pmi in zip(p, m, pm)]
    b = bs
    while b < c:
        sel = same(2 * b) & jnp.logical_not(same(b))
        off = [jnp.where(sel, l, 0.0) for l in ls]
        x = [oi + _mm(pi, oi) for pi, oi in zip(p, off)]
        p = [pi - (xi + _mm(xi, pi)) for pi, xi in zip(p, x)]
        b *= 2
    return p


def _gdn_body(act_ref, z_ref, bl_ref, al_ref, s0_ref, alog_ref, dtb_ref, gw_ref,
              o_ref, snew_ref, s_ref, u_ref, wq_ref, qk_ref, kdec_ref, *, tb, c, n_valid):
    t = pl.program_id(1)

    @pl.when(t == 0)
    def _():
        s_ref[...] = s0_ref[...]

    beta_all = jax.nn.sigmoid(bl_ref[...])
    a_in = al_ref[...] + dtb_ref[...]
    softplus = jnp.maximum(a_in, 0.0) + jnp.log1p(jnp.exp(-jnp.abs(a_in)))
    g_all = -jnp.exp(alog_ref[...]) * softplus
    if n_valid < tb:
        live = lax.broadcasted_iota(jnp.int32, (tb, GATE_W), 0) < n_valid
        beta_all = jnp.where(live, beta_all, 0.0)
        g_all = jnp.where(live, g_all, 0.0)

    row = lax.broadcasted_iota(jnp.int32, (c, c), 0)
    col = lax.broadcasted_iota(jnp.int32, (c, c), 1)
    causal = row >= col
    strict = row > col
    tril = jnp.where(causal, 1.0, 0.0).astype(BF16)
    gw = gw_ref[...]
    n_chunks = tb // c

    g_tots = []
    lmats, rhss = [], []
    for ci in range(n_chunks):
        r0 = ci * c
        beta = beta_all[r0:r0 + c, :]
        g_cum = _chunk_cumsum(g_all[r0:r0 + c, :], tril)
        g_cum_t = g_cum.T
        g_last = g_cum[c - 1:c, :]
        e_g = jnp.exp(g_cum)
        e_gl = jnp.exp(g_last - g_cum)
        g_tots.append(jnp.exp(g_last))
        for h in range(A_HEADS):
            i = ci * A_HEADS + h
            q = act_ref[r0:r0 + c, h * A_DK:(h + 1) * A_DK]
            k = act_ref[r0:r0 + c, A_QK + h * A_DK:A_QK + (h + 1) * A_DK]
            v = act_ref[r0:r0 + c, 2 * A_QK + h * A_DV:2 * A_QK + (h + 1) * A_DV]
            q = q * lax.rsqrt(jnp.sum(q * q, axis=-1, keepdims=True) + EPS) * (A_DK ** -0.5)
            k = k * lax.rsqrt(jnp.sum(k * k, axis=-1, keepdims=True) + EPS)
            b_col = beta[:, h:h + 1]
            eg_col = e_g[:, h:h + 1]
            decay = jnp.exp(jnp.where(causal, g_cum[:, h:h + 1] - g_cum_t[h:h + 1, :], -jnp.inf))
            k16 = k.astype(BF16)
            qk_kk = lax.dot_general(jnp.concatenate([q.astype(BF16), k16], axis=0), k16,
                                    (((1,), (1,)), ((), ())), preferred_element_type=F32)
            qk_ref[i] = (qk_kk[0:c, :] * decay).astype(BF16)
            lmats.append(jnp.where(strict, qk_kk[c:2 * c, :] * decay * b_col, 0.0))
            rhss.append(jnp.concatenate([v * b_col, k * (b_col * eg_col)], axis=1))
            wq_ref[i, c:2 * c, :] = (q * eg_col).astype(BF16)
            kdec_ref[i] = (k * e_gl[:, h:h + 1]).astype(BF16)
    tinvs = _unit_lower_inverses_minus_eye(lmats, c)
    for i, (tinv, rhs) in enumerate(zip(tinvs, rhss)):
        sol = rhs + _mm(tinv, rhs)
        u_ref[i] = sol[:, 0:A_DV]
        wq_ref[i, 0:c, :] = sol[:, A_DV:A_DV + A_DK].astype(BF16)

    for ci in range(n_chunks):
        r0 = ci * c
        idx = [ci * A_HEADS + h for h in range(A_HEADS)]
        s_old = [s_ref[h] for h in range(A_HEADS)]
        wq_s = [jnp.dot(wq_ref[i], s.astype(BF16), preferred_element_type=F32) for i, s in zip(idx, s_old)]
        v_new = [u_ref[i] - x[0:c, :] for i, x in zip(idx, wq_s)]
        v16 = [x.astype(BF16) for x in v_new]
        for h in range(A_HEADS):
            s_ref[h] = s_old[h] * g_tots[ci][:, h:h + 1] + lax.dot_general(
                kdec_ref[idx[h]], v16[h], (((0,), (0,)), ((), ())), preferred_element_type=F32)
        for h in range(A_HEADS):
            o = wq_s[h][c:2 * c, :] + jnp.dot(qk_ref[idx[h]], v16[h], preferred_element_type=F32)
            o = o * lax.rsqrt(jnp.mean(o * o, axis=-1, keepdims=True) + EPS) * gw
            o_ref[r0:r0 + c, h * A_DV:(h + 1) * A_DV] = o * _silu(z_ref[r0:r0 + c, h * A_DV:(h + 1) * A_DV])

    @pl.when(t == pl.num_programs(1) - 1)
    def _():
        snew_ref[...] = s_ref[...]


def _gdn(act, z, bl, al, s0, alog, dtb, gw, *, tb, c, n_valid):
    b, l, _ = act.shape
    nt = l // tb
    assert n_valid == tb or nt == 1
    n_items = (tb // c) * A_HEADS
    blk = lambda width: pl.BlockSpec((None, tb, width), lambda i, j: (i, j, 0))
    state = pl.BlockSpec((None, A_HEADS, A_DK, A_DV), lambda i, j: (i, 0, 0, 0))
    lane_row = pl.BlockSpec((1, LANES), lambda i, j: (0, 0))
    return pl.pallas_call(
        functools.partial(_gdn_body, tb=tb, c=c, n_valid=n_valid),
        grid=(b, nt),
        in_specs=[blk(A_QKV), blk(A_VW), blk(GATE_W), blk(GATE_W), state, lane_row, lane_row, lane_row],
        out_specs=[blk(A_VW), state],
        out_shape=[jax.ShapeDtypeStruct((b, l, A_VW), F32),
                   jax.ShapeDtypeStruct((b, A_HEADS, A_DK, A_DV), F32)],
        scratch_shapes=[pltpu.VMEM((A_HEADS, A_DK, A_DV), F32),
                        pltpu.VMEM((n_items, c, A_DV), F32),
                        pltpu.VMEM((n_items, 2 * c, A_DK), BF16),
                        pltpu.VMEM((n_items, c, c), BF16),
                        pltpu.VMEM((n_items, c, A_DK), BF16)],
        compiler_params=_params(("parallel", "arbitrary")),
        name="gdn",
    )(act, z, bl, al, s0, alog, dtb, gw)


def _rope_table_body(inv_ref, ma_ref, mb_ref, cos_ref, sa_ref, sb_ref, *, tm, pos0):
    pos = pos0 + pl.program_id(0) * tm + lax.broadcasted_iota(jnp.int32, (tm, LANES), 0)
    ang = pos.astype(F32) * inv_ref[...]
    sin = jnp.sin(ang)
    cos_ref[...] = jnp.cos(ang)
    sa_ref[...] = sin * ma_ref[...]
    sb_ref[...] = -sin * mb_ref[...]


def _rope_tables(inv_row, ma, mb, *, length, pos0, tm):
    row = pl.BlockSpec((1, LANES), lambda i: (0, 0))
    out = pl.BlockSpec((tm, LANES), lambda i: (i, 0))
    return pl.pallas_call(
        functools.partial(_rope_table_body, tm=tm, pos0=pos0),
        grid=(length // tm,),
        in_specs=[row, row, row],
        out_specs=[out, out, out],
        out_shape=[jax.ShapeDtypeStruct((length, LANES), F32)] * 3,
        compiler_params=_params(("parallel",)),
        name="rope_tables",
    )(inv_row, ma, mb)


SWA_LOCKSTEP = 4


def _swa_body(qb_ref, kb_ref, vb_ref, cos_ref, sa_ref, sb_ref, kc_ref, vc_ref, sinks_ref,
              o_ref, newk_ref, newv_ref, kx_ref, vx_ref, qs_ref, kvar_ref, vvar_ref,
              *, tb, c, mask_start):
    t = pl.program_id(1)
    half = B_HD
    rot = ROPE_DIM // 2
    w = WINDOW + c

    @pl.when(t == 0)
    def _():
        kx_ref[0:WINDOW, :] = kc_ref[...]
        vx_ref[0:WINDOW, :] = vc_ref[...]

    cos = cos_ref[...]
    sa = sa_ref[...]
    sb = sb_ref[...]

    def rope(x):
        return x * cos + pltpu.roll(x, rot, 1) * sa + pltpu.roll(x, LANES - rot, 1) * sb

    kx_ref[WINDOW:WINDOW + tb, :] = rope(kb_ref[...])
    vx_ref[WINDOW:WINDOW + tb, :] = vb_ref[...]
    for s in range(B_QW // LANES):
        qs_ref[:, s * LANES:(s + 1) * LANES] = (
            rope(qb_ref[:, s * LANES:(s + 1) * LANES]) * (B_HD ** -0.5)).astype(BF16)

    lo = lax.broadcasted_iota(jnp.int32, (WINDOW + tb, LANES), 1) < half
    for src, dst in ((kx_ref, kvar_ref), (vx_ref, vvar_ref)):
        full = src[...]
        swapped = pltpu.roll(full, half, 1)
        dst[0] = jnp.where(lo, full, 0.0).astype(BF16)
        dst[1] = jnp.where(lo, 0.0, swapped).astype(BF16)
        dst[2] = jnp.where(lo, swapped, 0.0).astype(BF16)
        dst[3] = jnp.where(lo, 0.0, full).astype(BF16)

    first_rows = lax.broadcasted_iota(jnp.int32, (2 * c, 1), 0) < c
    kcol = lax.broadcasted_iota(jnp.int32, (2 * c, w), 1)

    sks = [jnp.where(first_rows, sinks_ref[4 * g + hh], sinks_ref[4 * g + 2 + hh])
           for g in range(B_KV_HEADS) for hh in range(2)]
    n_chunks = tb // c
    for i0 in range(0, n_chunks, SWA_LOCKSTEP):
        items = [(i, g, hh) for i in range(i0, min(i0 + SWA_LOCKSTEP, n_chunks))
                 for g in range(B_KV_HEADS) for hh in range(2)]
        lhs = {(i, g): jnp.concatenate([qs_ref[i * c:(i + 1) * c, (2 * g) * LANES:(2 * g + 1) * LANES],
                                        qs_ref[i * c:(i + 1) * c, (2 * g + 1) * LANES:(2 * g + 2) * LANES]], axis=0)
               for i, g, hh in items if hh == 0}
        s = [lax.dot_general(lhs[i, g], kvar_ref[2 * g + hh, i * c:i * c + w, :],
                             (((1,), (1,)), ((), ())), preferred_element_type=F32) for i, g, hh in items]
        if mask_start:
            s = [jnp.where(t * tb + i * c - WINDOW + kcol >= 0, si, -jnp.inf) for si, (i, g, hh) in zip(s, items)]
        m = [jnp.maximum(jnp.max(si, axis=-1, keepdims=True), sks[2 * g + hh]) for si, (i, g, hh) in zip(s, items)]
        p = [jnp.exp(si - mi) for si, mi in zip(s, m)]
        den = [jnp.sum(pi, axis=-1, keepdims=True) + jnp.exp(sks[2 * g + hh] - mi)
               for pi, mi, (i, g, hh) in zip(p, m, items)]
        pv = [jnp.dot(pi.astype(BF16), vvar_ref[2 * g + hh, i * c:i * c + w, :], preferred_element_type=F32) / di
              for pi, di, (i, g, hh) in zip(p, den, items)]
        for j in range(0, len(items), 2):
            i, g, _ = items[j]
            o = pv[j] + pv[j + 1]
            o_ref[i * c:(i + 1) * c, (2 * g) * LANES:(2 * g + 1) * LANES] = o[0:c, :]
            o_ref[i * c:(i + 1) * c, (2 * g + 1) * LANES:(2 * g + 2) * LANES] = o[c:2 * c, :]

    @pl.when(t == pl.num_programs(1) - 1)
    def _():
        newk_ref[...] = kx_ref[tb:tb + WINDOW, :]
        newv_ref[...] = vx_ref[tb:tb + WINDOW, :]

    if tb >= WINDOW:
        kx_ref[0:WINDOW, :] = kx_ref[tb:tb + WINDOW, :]
        vx_ref[0:WINDOW, :] = vx_ref[tb:tb + WINDOW, :]


def _swa(qb, kb, vb, cos, sa, sb, kc, vc, sinks, *, tb, c, mask_start):
    b, l, _ = qb.shape
    nt = l // tb
    assert tb >= WINDOW or nt == 1
    blk = lambda width: pl.BlockSpec((None, tb, width), lambda i, j: (i, j, 0))
    tab = pl.BlockSpec((tb, LANES), lambda i, j: (j, 0))
    cache = pl.BlockSpec((None, WINDOW, B_KW), lambda i, j: (i, 0, 0))
    return pl.pallas_call(
        functools.partial(_swa_body, tb=tb, c=c, mask_start=mask_start),
        grid=(b, nt),
        in_specs=[blk(B_QW), blk(B_KW), blk(B_KW), tab, tab, tab, cache, cache,
                  pl.BlockSpec(memory_space=pltpu.SMEM)],
        out_specs=[blk(B_QW), cache, cache],
        out_shape=[jax.ShapeDtypeStruct((b, l, B_QW), F32),
                   jax.ShapeDtypeStruct((b, WINDOW, B_KW), F32),
                   jax.ShapeDtypeStruct((b, WINDOW, B_KW), F32)],
        scratch_shapes=[pltpu.VMEM((WINDOW + tb, B_KW), F32),
                        pltpu.VMEM((WINDOW + tb, B_KW), F32),
                        pltpu.VMEM((tb, B_QW), BF16),
                        pltpu.VMEM((4, WINDOW + tb, LANES), BF16),
                        pltpu.VMEM((4, WINDOW + tb, LANES), BF16)],
        compiler_params=_params(("parallel", "arbitrary")),
        name="swa",
    )(qb, kb, vb, cos, sa, sb, kc, vc, sinks)


def _pack_w_in(w):
    offs = np.cumsum([0, A_QKV, A_VW, A_HEADS, A_HEADS, B_QW, B_KW, B_KW]).tolist()
    qkv, z, bg, ag, qb, kb, vb = [w[..., offs[i]:offs[i + 1]] for i in range(7)]
    pad = lambda g: jnp.pad(g, ((0, 0), (0, 0), (0, GATE_W - g.shape[-1])))
    return jnp.concatenate([qkv, z, pad(bg), pad(ag), qb, kb, vb], axis=-1).astype(BF16)


def _pad_lanes(v):
    return jnp.pad(v, ((0, 0), (0, GATE_W - v.shape[-1])))[:, None, :]


def _trunk(x, rope_tabs, conv_bufs, s0s, k_caches, v_caches, wts, *, batch, length, tm, proj_tm,
           gdn_tb, gdn_c, swa_tb, swa_c, mask_start):
    (norm_ff1, ff1_in, ff1_out, norm_mix, w_in, conv_w, alog, dtb, gnorm, sinks, w_out,
     norm_ff2, ff2_in, ff2_out, norm_final) = wts
    cos, sa, sb = rope_tabs
    lpad = -(-length // gdn_tb) * gdn_tb
    bufs, states, ks, vs = [], [], [], []

    def pad_rows(a):
        return a if lpad == length else jnp.pad(a, ((0, 0), (0, lpad - length), (0, 0)))

    for l in range(DEPTH):
        x = _ffn(x, norm_ff1[l], ff1_in[l], ff1_out[l], tm=tm)
        act, z, bg, ag, qb, kb, vb, nbuf = _proj_in(x.reshape(batch, length, D_MODEL), norm_mix[l], w_in[l],
                                                    conv_bufs[l], conv_w[l], tm=proj_tm)
        o_a, s_new = _gdn(pad_rows(act), pad_rows(z), pad_rows(bg), pad_rows(ag), s0s[l], alog[l], dtb[l], gnorm[l],
                          tb=gdn_tb, c=gdn_c, n_valid=min(length, gdn_tb))
        o_b, nk, nv = _swa(qb, kb, vb, cos, sa, sb, k_caches[l], v_caches[l], sinks[l],
                           tb=swa_tb, c=swa_c, mask_start=mask_start)
        o_a = o_a[:, :length].reshape(batch * length, A_VW)
        x = _mix_out_ffn(x, o_a, o_b.reshape(batch * length, B_QW), w_out[l], norm_ff2[l], ff2_in[l], ff2_out[l],
                         norm_final if l == DEPTH - 1 else None, tm=tm)
        bufs.append(nbuf)
        states.append(s_new)
        ks.append(nk.reshape(batch, WINDOW, B_KV_HEADS, B_HD))
        vs.append(nv.reshape(batch, WINDOW, B_KV_HEADS, B_HD))
    y = x.reshape(batch, length, D_MODEL)
    return y, jnp.stack(bufs), jnp.stack(states), jnp.stack(ks), jnp.stack(vs)


def kernel(x_prompt, x_sample, cache_conv, state_delta, cache_k, cache_v, norm_ff1, ff1_w_in, ff1_w_out, norm_mix, w_mix_in, conv_w, a_log, dt_bias, gnorm_w, sinks, w_mix_out, norm_ff2, ff2_w_in, ff2_w_out, norm_final):
    bp, lp, _ = x_prompt.shape
    bs, ls, _ = x_sample.shape
    rows = cache_k.shape[2]
    assert rows == WINDOW

    wts = (norm_ff1[:, None, :], ff1_w_in.astype(BF16), ff1_w_out.astype(BF16), norm_mix[:, None, :],
           _pack_w_in(w_mix_in), conv_w, _pad_lanes(a_log), _pad_lanes(dt_bias), gnorm_w[:, None, :], sinks,
           w_mix_out.astype(BF16), norm_ff2[:, None, :], ff2_w_in.astype(BF16), ff2_w_out.astype(BF16),
           norm_final[None, :])

    inv = jnp.power(ROPE_THETA, -jnp.arange(0, ROPE_DIM, 2, dtype=F32) / ROPE_DIM)
    rot = ROPE_DIM // 2
    head_row = jnp.concatenate([inv, inv, jnp.zeros((B_HD - ROPE_DIM,), F32)])
    inv_row = jnp.tile(head_row, LANES // B_HD)[None, :]
    d = np.arange(LANES) % B_HD
    ma = jnp.asarray(((d >= rot) & (d < ROPE_DIM)).astype(np.float32))[None, :]
    mb = jnp.asarray((d < rot).astype(np.float32))[None, :]

    zero_buf = jnp.zeros((DEPTH, bp, CONV_W - 1, A_QKV), F32)
    zero_state = jnp.zeros((DEPTH, bp, A_HEADS, A_DK, A_DV), F32)
    zero_kv = jnp.zeros((DEPTH, bp, WINDOW, B_KW), F32)
    tabs_p = _rope_tables(inv_row, ma, mb, length=lp, pos0=0, tm=512)
    y_p, conv_p, delta_p, k_p, v_p = _trunk(
        x_prompt.reshape(bp * lp, D_MODEL), tabs_p, zero_buf, zero_state, zero_kv, zero_kv, wts,
        batch=bp, length=lp, tm=512, proj_tm=512, gdn_tb=512, gdn_c=128, swa_tb=512, swa_c=64, mask_start=True)

    tabs_s = _rope_tables(inv_row, ma, mb, length=ls, pos0=PAST_LEN, tm=ls)
    y_s, conv_s, delta_s, k_s, v_s = _trunk(
        x_sample.reshape(bs * ls, D_MODEL), tabs_s, cache_conv, state_delta,
        cache_k.reshape(DEPTH, bs, WINDOW, B_KW), cache_v.reshape(DEPTH, bs, WINDOW, B_KW), wts,
        batch=bs, length=ls, tm=bs * ls, proj_tm=ls, gdn_tb=128, gdn_c=128, swa_tb=ls, swa_c=ls, mask_start=False)
    return (y_p, y_s, conv_p, delta_p, k_p, v_p, conv_s, delta_s, k_s, v_s)
```

```python
import functools

import numpy as np
import jax
import jax.numpy as jnp
from jax import lax
from jax.experimental import pallas as pl
from jax.experimental.pallas import tpu as pltpu

F32 = jnp.float32
BF16 = jnp.bfloat16

D_MODEL = 1024
DEPTH = 4
PAST_LEN = 4096
EPS = 1e-6
A_HEADS = 4
A_DK = 128
A_DV = 128
A_QK = A_HEADS * A_DK
A_VW = A_HEADS * A_DV
A_QKV = 2 * A_QK + A_VW
CONV_W = 4
B_HEADS = 8
B_KV_HEADS = 2
B_HD = 64
B_QW = B_HEADS * B_HD
B_KW = B_KV_HEADS * B_HD
WINDOW = 128
ROPE_DIM = B_HD // 4
ROPE_THETA = 500000.0
D_MIX = A_VW + B_QW
D_FF = 2816

LANES = 128
SUBLANES = 8
GATE_W = LANES
VMEM_LIMIT = 56 * 1024 * 1024


def _params(sem, vmem=VMEM_LIMIT):
    return pltpu.CompilerParams(dimension_semantics=sem, vmem_limit_bytes=vmem)


def _resident(shape):
    nd = len(shape)
    return pl.BlockSpec(shape, lambda *_: (0,) * nd, pipeline_mode=pl.Buffered(1))


def _rms(x, w):
    return x * lax.rsqrt(jnp.mean(x * x, axis=-1, keepdims=True) + EPS) * w


def _silu(x):
    return x * jax.nn.sigmoid(x)


def _mm(a, b):
    return jnp.dot(a.astype(BF16), b.astype(BF16), preferred_element_type=F32)


def _swiglu_residual(x, nw_ref, win_ref, wout_ref, fc):
    h = _rms(x, nw_ref[...]).astype(BF16)
    acc = None
    for c in range(0, D_FF, fc):
        gate = jnp.dot(h, win_ref[:, c:c + fc], preferred_element_type=F32)
        up = jnp.dot(h, win_ref[:, D_FF + c:D_FF + c + fc], preferred_element_type=F32)
        act = (_silu(gate) * up).astype(BF16)
        part = jnp.dot(act, wout_ref[c:c + fc, :], preferred_element_type=F32)
        acc = part if acc is None else acc + part
    return x + 0.5 * acc


def _ffn_body(x_ref, nw_ref, win_ref, wout_ref, o_ref, *, fc):
    o_ref[...] = _swiglu_residual(x_ref[...], nw_ref, win_ref, wout_ref, fc)


def _ffn(x, nw, w_in, w_out, *, tm, fc=2816):
    m = x.shape[0]
    return pl.pallas_call(
        functools.partial(_ffn_body, fc=fc),
        grid=(m // tm,),
        in_specs=[pl.BlockSpec((tm, D_MODEL), lambda i: (i, 0)),
                  _resident((1, D_MODEL)),
                  _resident((D_MODEL, 2 * D_FF)),
                  _resident((D_FF, D_MODEL))],
        out_specs=pl.BlockSpec((tm, D_MODEL), lambda i: (i, 0)),
        out_shape=jax.ShapeDtypeStruct((m, D_MODEL), F32),
        compiler_params=_params(("parallel",)),
        name="ffn",
    )(x, nw, w_in, w_out)


def _mix_out_ffn_body(x_ref, oa_ref, ob_ref, wo_ref, nw_ref, win_ref, wout_ref, *rest, fc, final):
    x = (x_ref[...]
         + jnp.dot(oa_ref[...].astype(BF16), wo_ref[0:A_VW, :], preferred_element_type=F32)
         + jnp.dot(ob_ref[...].astype(BF16), wo_ref[A_VW:D_MIX, :], preferred_element_type=F32))
    y = _swiglu_residual(x, nw_ref, win_ref, wout_ref, fc)
    if final:
        nf_ref, o_ref = rest
        o_ref[...] = _rms(y, nf_ref[...])
    else:
        (o_ref,) = rest
        o_ref[...] = y


def _mix_out_ffn(x, oa, ob, wo, nw, w_in, w_out, norm_final=None, *, tm, fc=2816):
    m = x.shape[0]
    final = norm_final is not None
    row = lambda width: pl.BlockSpec((tm, width), lambda i: (i, 0))
    in_specs = [row(D_MODEL), row(A_VW), row(B_QW), _resident((D_MIX, D_MODEL)), _resident((1, D_MODEL)),
                _resident((D_MODEL, 2 * D_FF)), _resident((D_FF, D_MODEL))]
    args = [x, oa, ob, wo, nw, w_in, w_out]
    if final:
        in_specs.append(_resident((1, D_MODEL)))
        args.append(norm_final)
    return pl.pallas_call(
        functools.partial(_mix_out_ffn_body, fc=fc, final=final),
        grid=(m // tm,),
        in_specs=in_specs,
        out_specs=row(D_MODEL),
        out_shape=jax.ShapeDtypeStruct((m, D_MODEL), F32),
        compiler_params=_params(("parallel",)),
        name="mix_out_ffn",
    )(*args)


O_TAIL = A_QKV + A_VW
_TAIL_GROUPS = (GATE_W, GATE_W, B_QW, B_KW, B_KW)
D_TAIL = sum(_TAIL_GROUPS)
HALO = CONV_W - 1
HALO_BASE = SUBLANES


def _proj_in_body(x_ref, nw_ref, w_ref, wt_ref, cbuf_ref, cw_ref, act_ref, z_ref, *rest, tm):
    tail_refs, (nbuf_ref, xp_ref) = rest[:-2], rest[-2:]
    j = pl.program_id(1)

    @pl.when(j == 0)
    def _():
        xp_ref[HALO_BASE - HALO:HALO_BASE, :] = cbuf_ref[...]

    h = _rms(x_ref[...], nw_ref[...]).astype(BF16)
    qkv = jnp.dot(h, w_ref[:, 0:A_QKV], preferred_element_type=F32)
    xp_ref[HALO_BASE:HALO_BASE + tm, :] = qkv
    z_ref[...] = jnp.dot(h, w_ref[:, A_QKV:O_TAIL], preferred_element_type=F32)
    off = 0
    for width, o_ref in zip(_TAIL_GROUPS, tail_refs):
        o_ref[...] = jnp.dot(h, wt_ref[:, off:off + width], preferred_element_type=F32)
        off += width

    cw = cw_ref[...]
    conv = qkv * cw[0:1, :]
    for t in range(1, CONV_W):
        conv = pltpu.roll(conv, 1, 0) + qkv * cw[t:t + 1, :]
    act_ref[...] = _silu(conv)
    lo = HALO_BASE - HALO
    head = xp_ref[lo:lo + SUBLANES, :] * cw[0:1, :]
    for t in range(1, CONV_W):
        head = head + xp_ref[lo + t:lo + t + SUBLANES, :] * cw[t:t + 1, :]
    act_ref[0:SUBLANES, :] = _silu(head)

    @pl.when(j == pl.num_programs(1) - 1)
    def _():
        nbuf_ref[...] = xp_ref[HALO_BASE + tm - HALO:HALO_BASE + tm, :]

    xp_ref[HALO_BASE - HALO:HALO_BASE, :] = xp_ref[HALO_BASE + tm - HALO:HALO_BASE + tm, :]


def _proj_in(x, nw, w, w_tail, cbuf, cw, *, tm):
    b, l, _ = x.shape
    blk = lambda width: pl.BlockSpec((None, tm, width), lambda i, j: (i, j, 0))
    buf = pl.BlockSpec((None, HALO, A_QKV), lambda i, j: (i, 0, 0))
    widths = [A_QKV, A_VW] + list(_TAIL_GROUPS)
    return pl.pallas_call(
        functools.partial(_proj_in_body, tm=tm),
        grid=(b, l // tm),
        in_specs=[blk(D_MODEL), _resident((1, D_MODEL)), _resident((D_MODEL, O_TAIL)),
                  _resident((D_MODEL, D_TAIL)), buf, pl.BlockSpec((CONV_W, A_QKV), lambda i, j: (0, 0))],
        out_specs=[blk(width) for width in widths] + [buf],
        out_shape=[jax.ShapeDtypeStruct((b, l, width), F32) for width in widths]
                  + [jax.ShapeDtypeStruct((b, HALO, A_QKV), F32)],
        scratch_shapes=[pltpu.VMEM((HALO_BASE + tm, A_QKV), F32)],
        compiler_params=_params(("parallel", "arbitrary")),
        name="proj_in",
    )(x, nw, w, w_tail, cbuf, cw)


def _chunk_cumsum(g, tril):
    g1 = g.astype(BF16)
    r1 = g - g1.astype(F32)
    g2 = r1.astype(BF16)
    g3 = (r1 - g2.astype(F32)).astype(BF16)
    parts = jnp.dot(tril, jnp.concatenate([g1, g2, g3], axis=1), preferred_element_type=F32)
    return parts[:, 0:LANES] + parts[:, LANES:2 * LANES] + parts[:, 2 * LANES:3 * LANES]


INV_BLOCK = 32


def _unit_lower_inverses_minus_eye(ls, c):
    bs = min(INV_BLOCK, c)
    row = lax.broadcasted_iota(jnp.int32, (c, c), 0)
    col = lax.broadcasted_iota(jnp.int32, (c, c), 1)
    same = lambda b: (row // b) == (col // b)
    diag = [jnp.where(same(bs), l, 0.0) for l in ls] if bs < c else ls
    p = [-d for d in diag]
    m = [_mm(d, d) for d in diag]
    for _ in range(int(np.log2(bs)) - 2):
        pm = [_mm(pi, mi) for pi, mi in zip(p, m)]
        p = [pi + mi + pmi for pi, mi, pmi in zip(p, m, pm)]
        m = [_mm(mi, mi) for mi in m]
    pm = [_mm(pi, mi) for pi, mi in zip(p, m)]
    p = [pi + mi + pmi for pi, mi, pmi in zip(p, m, pm)]
    b = bs
    while b < c:
        sel = same(2 * b) & jnp.logical_not(same(b))
        off = [jnp.where(sel, l, 0.0) for l in ls]
        x = [oi + _mm(pi, oi) for pi, oi in zip(p, off)]
        p = [pi - (xi + _mm(xi, pi)) for pi, xi in zip(p, x)]
        b *= 2
    return p


def _gdn_body(act_ref, z_ref, bl_ref, al_ref, s0_ref, alog_ref, dtb_ref, gw_ref,
              o_ref, snew_ref, s_ref, u_ref, wq_ref, qk_ref, kdec_ref, *, tb, c, n_valid):
    t = pl.program_id(1)

    @pl.when(t == 0)
    def _():
        s_ref[...] = s0_ref[...]

    beta_all = jax.nn.sigmoid(bl_ref[...])
    a_in = al_ref[...] + dtb_ref[...]
    softplus = jnp.maximum(a_in, 0.0) + jnp.log1p(jnp.exp(-jnp.abs(a_in)))
    g_all = -jnp.exp(alog_ref[...]) * softplus
    if n_valid < tb:
        live = lax.broadcasted_iota(jnp.int32, (tb, GATE_W), 0) < n_valid
        beta_all = jnp.where(live, beta_all, 0.0)
        g_all = jnp.where(live, g_all, 0.0)

    row = lax.broadcasted_iota(jnp.int32, (c, c), 0)
    col = lax.broadcasted_iota(jnp.int32, (c, c), 1)
    causal = row >= col
    strict = row > col
    tril = jnp.where(causal, 1.0, 0.0).astype(BF16)
    gw = gw_ref[...]
    n_chunks = tb // c

    g_tots = []
    lmats, rhss = [], []
    for ci in range(n_chunks):
        r0 = ci * c
        beta = beta_all[r0:r0 + c, :]
        g_cum = _chunk_cumsum(g_all[r0:r0 + c, :], tril)
        g_cum_t = g_cum.T
        g_last = g_cum[c - 1:c, :]
        e_g = jnp.exp(g_cum)
        e_gl = jnp.exp(g_last - g_cum)
        g_tots.append(jnp.exp(g_last))
        for h in range(A_HEADS):
            i = ci * A_HEADS + h
            q = act_ref[r0:r0 + c, h * A_DK:(h + 1) * A_DK]
            k = act_ref[r0:r0 + c, A_QK + h * A_DK:A_QK + (h + 1) * A_DK]
            v = act_ref[r0:r0 + c, 2 * A_QK + h * A_DV:2 * A_QK + (h + 1) * A_DV]
            q = q * lax.rsqrt(jnp.sum(q * q, axis=-1, keepdims=True) + EPS) * (A_DK ** -0.5)
            k = k * lax.rsqrt(jnp.sum(k * k, axis=-1, keepdims=True) + EPS)
            b_col = beta[:, h:h + 1]
            eg_col = e_g[:, h:h + 1]
            decay = jnp.exp(jnp.where(causal, g_cum[:, h:h + 1] - g_cum_t[h:h + 1, :], -jnp.inf))
            k16 = k.astype(BF16)
            qk_kk = lax.dot_general(jnp.concatenate([q.astype(BF16), k16], axis=0), k16,
                                    (((1,), (1,)), ((), ())), preferred_element_type=F32)
            qk_ref[i] = (qk_kk[0:c, :] * decay).astype(BF16)
            lmats.append(jnp.where(strict, qk_kk[c:2 * c, :] * decay * b_col, 0.0))
            rhss.append(jnp.concatenate([v * b_col, k * (b_col * eg_col)], axis=1))
            wq_ref[i, c:2 * c, :] = (q * eg_col).astype(BF16)
            kdec_ref[i] = (k * e_gl[:, h:h + 1]).astype(BF16)
    tinvs = _unit_lower_inverses_minus_eye(lmats, c)
    for i, (tinv, rhs) in enumerate(zip(tinvs, rhss)):
        sol = rhs + _mm(tinv, rhs)
        u_ref[i] = sol[:, 0:A_DV]
        wq_ref[i, 0:c, :] = sol[:, A_DV:A_DV + A_DK].astype(BF16)

    for ci in range(n_chunks):
        r0 = ci * c
        idx = [ci * A_HEADS + h for h in range(A_HEADS)]
        s_old = [s_ref[h] for h in range(A_HEADS)]
        wq_s = [jnp.dot(wq_ref[i], s.astype(BF16), preferred_element_type=F32) for i, s in zip(idx, s_old)]
        v_new = [u_ref[i] - x[0:c, :] for i, x in zip(idx, wq_s)]
        v16 = [x.astype(BF16) for x in v_new]
        for h in range(A_HEADS):
            s_ref[h] = s_old[h] * g_tots[ci][:, h:h + 1] + lax.dot_general(
                kdec_ref[idx[h]], v16[h], (((0,), (0,)), ((), ())), preferred_element_type=F32)
        for h in range(A_HEADS):
            o = wq_s[h][c:2 * c, :] + jnp.dot(qk_ref[idx[h]], v16[h], preferred_element_type=F32)
            o = o * lax.rsqrt(jnp.mean(o * o, axis=-1, keepdims=True) + EPS) * gw
            o_ref[r0:r0 + c, h * A_DV:(h + 1) * A_DV] = o * _silu(z_ref[r0:r0 + c, h * A_DV:(h + 1) * A_DV])

    @pl.when(t == pl.num_programs(1) - 1)
    def _():
        snew_ref[...] = s_ref[...]


def _gdn(act, z, bl, al, s0, alog, dtb, gw, *, tb, c, n_valid):
    b, l, _ = act.shape
    nt = l // tb
    assert n_valid == tb or nt == 1
    n_items = (tb // c) * A_HEADS
    blk = lambda width: pl.BlockSpec((None, tb, width), lambda i, j: (i, j, 0))
    state = pl.BlockSpec((None, A_HEADS, A_DK, A_DV), lambda i, j: (i, 0, 0, 0))
    lane_row = pl.BlockSpec((1, LANES), lambda i, j: (0, 0))
    return pl.pallas_call(
        functools.partial(_gdn_body, tb=tb, c=c, n_valid=n_valid),
        grid=(b, nt),
        in_specs=[blk(A_QKV), blk(A_VW), blk(GATE_W), blk(GATE_W), state, lane_row, lane_row, lane_row],
        out_specs=[blk(A_VW), state],
        out_shape=[jax.ShapeDtypeStruct((b, l, A_VW), F32),
                   jax.ShapeDtypeStruct((b, A_HEADS, A_DK, A_DV), F32)],
        scratch_shapes=[pltpu.VMEM((A_HEADS, A_DK, A_DV), F32),
                        pltpu.VMEM((n_items, c, A_DV), F32),
                        pltpu.VMEM((n_items, 2 * c, A_DK), BF16),
                        pltpu.VMEM((n_items, c, c), BF16),
                        pltpu.VMEM((n_items, c, A_DK), BF16)],
        compiler_params=_params(("parallel", "arbitrary")),
        name="gdn",
    )(act, z, bl, al, s0, alog, dtb, gw)


def _rope_table_body(inv_ref, ma_ref, mb_ref, cos_ref, sa_ref, sb_ref, *, tm, pos0):
    pos = pos0 + pl.program_id(0) * tm + lax.broadcasted_iota(jnp.int32, (tm, LANES), 0)
    ang = pos.astype(F32) * inv_ref[...]
    sin = jnp.sin(ang)
    cos_ref[...] = jnp.cos(ang)
    sa_ref[...] = sin * ma_ref[...]
    sb_ref[...] = -sin * mb_ref[...]


def _rope_tables(inv_row, ma, mb, *, length, pos0, tm):
    row = pl.BlockSpec((1, LANES), lambda i: (0, 0))
    out = pl.BlockSpec((tm, LANES), lambda i: (i, 0))
    return pl.pallas_call(
        functools.partial(_rope_table_body, tm=tm, pos0=pos0),
        grid=(length // tm,),
        in_specs=[row, row, row],
        out_specs=[out, out, out],
        out_shape=[jax.ShapeDtypeStruct((length, LANES), F32)] * 3,
        compiler_params=_params(("parallel",)),
        name="rope_tables",
    )(inv_row, ma, mb)


SWA_LOCKSTEP = 4


def _swa_body(qb_ref, kb_ref, vb_ref, cos_ref, sa_ref, sb_ref, kc_ref, vc_ref, sinks_ref,
              o_ref, newk_ref, newv_ref, kx_ref, vx_ref, qs_ref, kvar_ref, vvar_ref,
              *, tb, c, mask_start):
    t = pl.program_id(1)
    half = B_HD
    rot = ROPE_DIM // 2
    w = WINDOW + c

    @pl.when(t == 0)
    def _():
        kx_ref[0:WINDOW, :] = kc_ref[...]
        vx_ref[0:WINDOW, :] = vc_ref[...]

    cos = cos_ref[...]
    sa = sa_ref[...]
    sb = sb_ref[...]

    def rope(x):
        return x * cos + pltpu.roll(x, rot, 1) * sa + pltpu.roll(x, LANES - rot, 1) * sb

    kx_ref[WINDOW:WINDOW + tb, :] = rope(kb_ref[...])
    vx_ref[WINDOW:WINDOW + tb, :] = vb_ref[...]
    for s in range(B_QW // LANES):
        qs_ref[:, s * LANES:(s + 1) * LANES] = (
            rope(qb_ref[:, s * LANES:(s + 1) * LANES]) * (B_HD ** -0.5)).astype(BF16)

    lo = lax.broadcasted_iota(jnp.int32, (WINDOW + tb, LANES), 1) < half
    for src, dst in ((kx_ref, kvar_ref), (vx_ref, vvar_ref)):
        full = src[...]
        swapped = pltpu.roll(full, half, 1)
        dst[0] = jnp.where(lo, full, 0.0).astype(BF16)
        dst[1] = jnp.where(lo, 0.0, swapped).astype(BF16)
        dst[2] = jnp.where(lo, swapped, 0.0).astype(BF16)
        dst[3] = jnp.where(lo, 0.0, full).astype(BF16)

    first_rows = lax.broadcasted_iota(jnp.int32, (2 * c, 1), 0) < c
    kcol = lax.broadcasted_iota(jnp.int32, (2 * c, w), 1)

    sks = [jnp.where(first_rows, sinks_ref[4 * g + hh], sinks_ref[4 * g + 2 + hh])
           for g in range(B_KV_HEADS) for hh in range(2)]
    n_chunks = tb // c
    for i0 in range(0, n_chunks, SWA_LOCKSTEP):
        items = [(i, g, hh) for i in range(i0, min(i0 + SWA_LOCKSTEP, n_chunks))
                 for g in range(B_KV_HEADS) for hh in range(2)]
        lhs = {(i, g): jnp.concatenate([qs_ref[i * c:(i + 1) * c, (2 * g) * LANES:(2 * g + 1) * LANES],
                                        qs_ref[i * c:(i + 1) * c, (2 * g + 1) * LANES:(2 * g + 2) * LANES]], axis=0)
               for i, g, hh in items if hh == 0}
        s = [lax.dot_general(lhs[i, g], kvar_ref[2 * g + hh, i * c:i * c + w, :],
                             (((1,), (1,)), ((), ())), preferred_element_type=F32) for i, g, hh in items]
        if mask_start:
            s = [jnp.where(t * tb + i * c - WINDOW + kcol >= 0, si, -jnp.inf) for si, (i, g, hh) in zip(s, items)]
        m = [jnp.maximum(jnp.max(si, axis=-1, keepdims=True), sks[2 * g + hh]) for si, (i, g, hh) in zip(s, items)]
        p = [jnp.exp(si - mi) for si, mi in zip(s, m)]
        den = [jnp.sum(pi, axis=-1, keepdims=True) + jnp.exp(sks[2 * g + hh] - mi)
               for pi, mi, (i, g, hh) in zip(p, m, items)]
        pv = [jnp.dot(pi.astype(BF16), vvar_ref[2 * g + hh, i * c:i * c + w, :], preferred_element_type=F32) / di
              for pi, di, (i, g, hh) in zip(p, den, items)]
        for j in range(0, len(items), 2):
            i, g, _ = items[j]
            o = pv[j] + pv[j + 1]
            o_ref[i * c:(i + 1) * c, (2 * g) * LANES:(2 * g + 1) * LANES] = o[0:c, :]
            o_ref[i * c:(i + 1) * c, (2 * g + 1) * LANES:(2 * g + 2) * LANES] = o[c:2 * c, :]

    @pl.when(t == pl.num_programs(1) - 1)
    def _():
        newk_ref[...] = kx_ref[tb:tb + WINDOW, :]
        newv_ref[...] = vx_ref[tb:tb + WINDOW, :]

    if tb >= WINDOW:
        kx_ref[0:WINDOW, :] = kx_ref[tb:tb + WINDOW, :]
        vx_ref[0:WINDOW, :] = vx_ref[tb:tb + WINDOW, :]


def _swa(qb, kb, vb, cos, sa, sb, kc, vc, sinks, *, tb, c, mask_start):
    b, l, _ = qb.shape
    nt = l // tb
    assert tb >= WINDOW or nt == 1
    blk = lambda width: pl.BlockSpec((None, tb, width), lambda i, j: (i, j, 0))
    tab = pl.BlockSpec((tb, LANES), lambda i, j: (j, 0))
    cache = pl.BlockSpec((None, WINDOW, B_KW), lambda i, j: (i, 0, 0))
    return pl.pallas_call(
        functools.partial(_swa_body, tb=tb, c=c, mask_start=mask_start),
        grid=(b, nt),
        in_specs=[blk(B_QW), blk(B_KW), blk(B_KW), tab, tab, tab, cache, cache,
                  pl.BlockSpec(memory_space=pltpu.SMEM)],
        out_specs=[blk(B_QW), cache, cache],
        out_shape=[jax.ShapeDtypeStruct((b, l, B_QW), F32),
                   jax.ShapeDtypeStruct((b, WINDOW, B_KW), F32),
                   jax.ShapeDtypeStruct((b, WINDOW, B_KW), F32)],
        scratch_shapes=[pltpu.VMEM((WINDOW + tb, B_KW), F32),
                        pltpu.VMEM((WINDOW + tb, B_KW), F32),
                        pltpu.VMEM((tb, B_QW), BF16),
                        pltpu.VMEM((4, WINDOW + tb, LANES), BF16),
                        pltpu.VMEM((4, WINDOW + tb, LANES), BF16)],
        compiler_params=_params(("parallel", "arbitrary")),
        name="swa",
    )(qb, kb, vb, cos, sa, sb, kc, vc, sinks)


def _pack_w_in_tail(w):
    offs = np.cumsum([O_TAIL, A_HEADS, A_HEADS, B_QW, B_KW, B_KW]).tolist()
    bg, ag, qb, kb, vb = [w[..., offs[i]:offs[i + 1]] for i in range(5)]
    pad = lambda g: jnp.pad(g, ((0, 0), (0, 0), (0, GATE_W - g.shape[-1])))
    return jnp.concatenate([pad(bg), pad(ag), qb, kb, vb], axis=-1).astype(BF16)


def _pad_lanes(v):
    return jnp.pad(v, ((0, 0), (0, GATE_W - v.shape[-1])))[:, None, :]


def _trunk(x, rope_tabs, conv_bufs, s0s, k_caches, v_caches, wts, *, batch, length, tm, proj_tm,
           gdn_tb, gdn_c, swa_tb, swa_c, mask_start):
    (norm_ff1, ff1_in, ff1_out, norm_mix, w_in, w_in_tail, conv_w, alog, dtb, gnorm, sinks, w_out,
     norm_ff2, ff2_in, ff2_out, norm_final) = wts
    cos, sa, sb = rope_tabs
    lpad = -(-length // gdn_tb) * gdn_tb
    bufs, states, ks, vs = [], [], [], []

    def pad_rows(a):
        return a if lpad == length else jnp.pad(a, ((0, 0), (0, lpad - length), (0, 0)))

    for l in range(DEPTH):
        x = _ffn(x, norm_ff1[l], ff1_in[l], ff1_out[l], tm=tm)
        act, z, bg, ag, qb, kb, vb, nbuf = _proj_in(x.reshape(batch, length, D_MODEL), norm_mix[l], w_in[l],
                                                    w_in_tail[l], conv_bufs[l], conv_w[l], tm=proj_tm)
        o_a, s_new = _gdn(pad_rows(act), pad_rows(z), pad_rows(bg), pad_rows(ag), s0s[l], alog[l], dtb[l], gnorm[l],
                          tb=gdn_tb, c=gdn_c, n_valid=min(length, gdn_tb))
        o_b, nk, nv = _swa(qb, kb, vb, cos, sa, sb, k_caches[l], v_caches[l], sinks[l],
                           tb=swa_tb, c=swa_c, mask_start=mask_start)
        o_a = o_a[:, :length].reshape(batch * length, A_VW)
        x = _mix_out_ffn(x, o_a, o_b.reshape(batch * length, B_QW), w_out[l], norm_ff2[l], ff2_in[l], ff2_out[l],
                         norm_final if l == DEPTH - 1 else None, tm=tm)
        bufs.append(nbuf)
        states.append(s_new)
        ks.append(nk.reshape(batch, WINDOW, B_KV_HEADS, B_HD))
        vs.append(nv.reshape(batch, WINDOW, B_KV_HEADS, B_HD))
    y = x.reshape(batch, length, D_MODEL)
    return y, jnp.stack(bufs), jnp.stack(states), jnp.stack(ks), jnp.stack(vs)


def kernel(x_prompt, x_sample, cache_conv, state_delta, cache_k, cache_v, norm_ff1, ff1_w_in, ff1_w_out, norm_mix, w_mix_in, conv_w, a_log, dt_bias, gnorm_w, sinks, w_mix_out, norm_ff2, ff2_w_in, ff2_w_out, norm_final):
    bp, lp, _ = x_prompt.shape
    bs, ls, _ = x_sample.shape
    rows = cache_k.shape[2]
    assert rows == WINDOW

    wts = (norm_ff1[:, None, :], ff1_w_in.astype(BF16), ff1_w_out.astype(BF16), norm_mix[:, None, :],
           w_mix_in.astype(BF16), _pack_w_in_tail(w_mix_in), conv_w, _pad_lanes(a_log), _pad_lanes(dt_bias), gnorm_w[:, None, :], sinks,
           w_mix_out.astype(BF16), norm_ff2[:, None, :], ff2_w_in.astype(BF16), ff2_w_out.astype(BF16),
           norm_final[None, :])

    inv = jnp.power(ROPE_THETA, -jnp.arange(0, ROPE_DIM, 2, dtype=F32) / ROPE_DIM)
    rot = ROPE_DIM // 2
    head_row = jnp.concatenate([inv, inv, jnp.zeros((B_HD - ROPE_DIM,), F32)])
    inv_row = jnp.tile(head_row, LANES // B_HD)[None, :]
    d = np.arange(LANES) % B_HD
    ma = jnp.asarray(((d >= rot) & (d < ROPE_DIM)).astype(np.float32))[None, :]
    mb = jnp.asarray((d < rot).astype(np.float32))[None, :]

    zero_buf = jnp.zeros((DEPTH, bp, CONV_W - 1, A_QKV), F32)
    zero_state = jnp.zeros((DEPTH, bp, A_HEADS, A_DK, A_DV), F32)
    zero_kv = jnp.zeros((DEPTH, bp, WINDOW, B_KW), F32)
    tabs_p = _rope_tables(inv_row, ma, mb, length=lp, pos0=0, tm=512)
    y_p, conv_p, delta_p, k_p, v_p = _trunk(
        x_prompt.reshape(bp * lp, D_MODEL), tabs_p, zero_buf, zero_state, zero_kv, zero_kv, wts,
        batch=bp, length=lp, tm=512, proj_tm=512, gdn_tb=512, gdn_c=128, swa_tb=512, swa_c=64, mask_start=True)

    tabs_s = _rope_tables(inv_row, ma, mb, length=ls, pos0=PAST_LEN, tm=ls)
    y_s, conv_s, delta_s, k_s, v_s = _trunk(
        x_sample.reshape(bs * ls, D_MODEL), tabs_s, cache_conv, state_delta,
        cache_k.reshape(DEPTH, bs, WINDOW, B_KW), cache_v.reshape(DEPTH, bs, WINDOW, B_KW), wts,
        batch=bs, length=ls, tm=bs * ls, proj_tm=ls, gdn_tb=128, gdn_c=128, swa_tb=ls, swa_c=ls, mask_start=False)
    return (y_p, y_s, conv_p, delta_p, k_p, v_p, conv_s, delta_s, k_s, v_s)
```

```python
import functools

import numpy as np
import jax
import jax.numpy as jnp
from jax import lax
from jax.experimental import pallas as pl
from jax.experimental.pallas import tpu as pltpu

F32 = jnp.float32
BF16 = jnp.bfloat16

D_MODEL = 1024
DEPTH = 4
PAST_LEN = 4096
EPS = 1e-6
A_HEADS = 4
A_DK = 128
A_DV = 128
A_QK = A_HEADS * A_DK
A_VW = A_HEADS * A_DV
A_QKV = 2 * A_QK + A_VW
CONV_W = 4
B_HEADS = 8
B_KV_HEADS = 2
B_HD = 64
B_QW = B_HEADS * B_HD
B_KW = B_KV_HEADS * B_HD
WINDOW = 128
ROPE_DIM = B_HD // 4
ROPE_THETA = 500000.0
D_MIX = A_VW + B_QW
D_FF = 2816

LANES = 128
SUBLANES = 8
GATE_W = LANES
VMEM_LIMIT = 56 * 1024 * 1024


def _params(sem, vmem=VMEM_LIMIT):
    return pltpu.CompilerParams(dimension_semantics=sem, vmem_limit_bytes=vmem)


def _resident(shape, layer=None):
    nd = len(shape)
    if layer is None:
        return pl.BlockSpec(shape, lambda *_: (0,) * nd, pipeline_mode=pl.Buffered(1))
    return pl.BlockSpec((None,) + tuple(shape), lambda *_: (layer,) + (0,) * nd, pipeline_mode=pl.Buffered(1))


def _rms(x, w):
    return x * lax.rsqrt(jnp.mean(x * x, axis=-1, keepdims=True) + EPS) * w


def _silu(x):
    return x * jax.nn.sigmoid(x)


def _mm(a, b):
    return jnp.dot(a.astype(BF16), b.astype(BF16), preferred_element_type=F32)


def _swiglu_residual(x, nw_ref, win_ref, wout_ref, fc):
    h = _rms(x, nw_ref[...]).astype(BF16)
    acc = None
    for c in range(0, D_FF, fc):
        gate = jnp.dot(h, win_ref[:, c:c + fc], preferred_element_type=F32)
        up = jnp.dot(h, win_ref[:, D_FF + c:D_FF + c + fc], preferred_element_type=F32)
        act = (_silu(gate) * up).astype(BF16)
        part = jnp.dot(act, wout_ref[c:c + fc, :], preferred_element_type=F32)
        acc = part if acc is None else acc + part
    return x + 0.5 * acc


def _ffn_body(x_ref, nw_ref, win_ref, wout_ref, o_ref, *, fc):
    o_ref[...] = _swiglu_residual(x_ref[...], nw_ref, win_ref, wout_ref, fc)


def _ffn(x, nw, w_in, w_out, *, layer, tm, fc=2816):
    m = x.shape[0]
    return pl.pallas_call(
        functools.partial(_ffn_body, fc=fc),
        grid=(m // tm,),
        in_specs=[pl.BlockSpec((tm, D_MODEL), lambda i: (i, 0)),
                  _resident((1, D_MODEL), layer),
                  _resident((D_MODEL, 2 * D_FF), layer),
                  _resident((D_FF, D_MODEL), layer)],
        out_specs=pl.BlockSpec((tm, D_MODEL), lambda i: (i, 0)),
        out_shape=jax.ShapeDtypeStruct((m, D_MODEL), F32),
        compiler_params=_params(("parallel",)),
        name="ffn",
    )(x, nw, w_in, w_out)


def _mix_out_ffn_body(x_ref, oa_ref, ob_ref, wo_ref, nw_ref, win_ref, wout_ref, *rest, fc, final):
    x = (x_ref[...]
         + jnp.dot(oa_ref[...].astype(BF16), wo_ref[0:A_VW, :], preferred_element_type=F32)
         + jnp.dot(ob_ref[...].astype(BF16), wo_ref[A_VW:D_MIX, :], preferred_element_type=F32))
    y = _swiglu_residual(x, nw_ref, win_ref, wout_ref, fc)
    if final:
        nf_ref, o_ref = rest
        o_ref[...] = _rms(y, nf_ref[...])
    else:
        (o_ref,) = rest
        o_ref[...] = y


def _mix_out_ffn(x, oa, ob, wo, nw, w_in, w_out, norm_final=None, *, layer, tm, fc=2816):
    m = x.shape[0]
    final = norm_final is not None
    row = lambda width: pl.BlockSpec((tm, width), lambda i: (i, 0))
    in_specs = [row(D_MODEL), row(A_VW), row(B_QW), _resident((D_MIX, D_MODEL), layer),
                _resident((1, D_MODEL), layer), _resident((D_MODEL, 2 * D_FF), layer),
                _resident((D_FF, D_MODEL), layer)]
    args = [x, oa, ob, wo, nw, w_in, w_out]
    if final:
        in_specs.append(_resident((1, D_MODEL)))
        args.append(norm_final)
    return pl.pallas_call(
        functools.partial(_mix_out_ffn_body, fc=fc, final=final),
        grid=(m // tm,),
        in_specs=in_specs,
        out_specs=row(D_MODEL),
        out_shape=jax.ShapeDtypeStruct((m, D_MODEL), F32),
        compiler_params=_params(("parallel",)),
        name="mix_out_ffn",
    )(*args)


O_TAIL = A_QKV + A_VW
_TAIL_GROUPS = (GATE_W, GATE_W, B_QW, B_KW, B_KW)
D_TAIL = sum(_TAIL_GROUPS)
HALO = CONV_W - 1
HALO_BASE = SUBLANES


def _proj_in_body(x_ref, nw_ref, w_ref, wt_ref, cbuf_ref, cw_ref, act_ref, z_ref, *rest, tm):
    tail_refs, (nbuf_ref, xp_ref) = rest[:-2], rest[-2:]
    j = pl.program_id(1)

    @pl.when(j == 0)
    def _():
        xp_ref[HALO_BASE - HALO:HALO_BASE, :] = cbuf_ref[...]

    h = _rms(x_ref[...], nw_ref[...]).astype(BF16)
    qkv = jnp.dot(h, w_ref[:, 0:A_QKV], preferred_element_type=F32)
    xp_ref[HALO_BASE:HALO_BASE + tm, :] = qkv
    z_ref[...] = jnp.dot(h, w_ref[:, A_QKV:O_TAIL], preferred_element_type=F32)
    off = 0
    for width, o_ref in zip(_TAIL_GROUPS, tail_refs):
        o_ref[...] = jnp.dot(h, wt_ref[:, off:off + width], preferred_element_type=F32)
        off += width

    cw = cw_ref[...]
    conv = qkv * cw[0:1, :]
    for t in range(1, CONV_W):
        conv = pltpu.roll(conv, 1, 0) + qkv * cw[t:t + 1, :]
    act_ref[...] = _silu(conv)
    lo = HALO_BASE - HALO
    head = xp_ref[lo:lo + SUBLANES, :] * cw[0:1, :]
    for t in range(1, CONV_W):
        head = head + xp_ref[lo + t:lo + t + SUBLANES, :] * cw[t:t + 1, :]
    act_ref[0:SUBLANES, :] = _silu(head)

    @pl.when(j == pl.num_programs(1) - 1)
    def _():
        nbuf_ref[...] = xp_ref[HALO_BASE + tm - HALO:HALO_BASE + tm, :]

    xp_ref[HALO_BASE - HALO:HALO_BASE, :] = xp_ref[HALO_BASE + tm - HALO:HALO_BASE + tm, :]


def _proj_in(x, nw, w, w_tail, cbuf, cw, *, layer, tm):
    b, l, _ = x.shape
    blk = lambda width: pl.BlockSpec((None, tm, width), lambda i, j: (i, j, 0))
    buf = pl.BlockSpec((None, HALO, A_QKV), lambda i, j: (i, 0, 0))
    widths = [A_QKV, A_VW] + list(_TAIL_GROUPS)
    return pl.pallas_call(
        functools.partial(_proj_in_body, tm=tm),
        grid=(b, l // tm),
        in_specs=[blk(D_MODEL), _resident((1, D_MODEL), layer), _resident((D_MODEL, O_TAIL), layer),
                  _resident((D_MODEL, D_TAIL), layer), buf, _resident((CONV_W, A_QKV), layer)],
        out_specs=[blk(width) for width in widths] + [buf],
        out_shape=[jax.ShapeDtypeStruct((b, l, width), F32) for width in widths]
                  + [jax.ShapeDtypeStruct((b, HALO, A_QKV), F32)],
        scratch_shapes=[pltpu.VMEM((HALO_BASE + tm, A_QKV), F32)],
        compiler_params=_params(("parallel", "arbitrary")),
        name="proj_in",
    )(x, nw, w, w_tail, cbuf, cw)


def _chunk_cumsum(g, tril):
    g1 = g.astype(BF16)
    r1 = g - g1.astype(F32)
    g2 = r1.astype(BF16)
    g3 = (r1 - g2.astype(F32)).astype(BF16)
    parts = jnp.dot(tril, jnp.concatenate([g1, g2, g3], axis=1), preferred_element_type=F32)
    return parts[:, 0:LANES] + parts[:, LANES:2 * LANES] + parts[:, 2 * LANES:3 * LANES]


INV_BLOCK = 32


def _unit_lower_inverses_minus_eye(ls, c):
    bs = min(INV_BLOCK, c)
    row = lax.broadcasted_iota(jnp.int32, (c, c), 0)
    col = lax.broadcasted_iota(jnp.int32, (c, c), 1)
    same = lambda b: (row // b) == (col // b)
    diag = [jnp.where(same(bs), l, 0.0) for l in ls] if bs < c else ls
    p = [-d for d in diag]
    m = [_mm(d, d) for d in diag]
    for _ in range(int(np.log2(bs)) - 2):
        pm = [_mm(pi, mi) for pi, mi in zip(p, m)]
        p = [pi + mi + pmi for pi, mi, pmi in zip(p, m, pm)]
        m = [_mm(mi, mi) for mi in m]
    pm = [_mm(pi, mi) for pi, mi in zip(p, m)]
    p = [pi + mi + pmi for pi, mi, pmi in zip(p, m, pm)]
    b = bs
    while b < c:
        sel = same(2 * b) & jnp.logical_not(same(b))
        off = [jnp.where(sel, l, 0.0) for l in ls]
        x = [oi + _mm(pi, oi) for pi, oi in zip(p, off)]
        p = [pi - (xi + _mm(xi, pi)) for pi, xi in zip(p, x)]
        b *= 2
    return p


def _gdn_body(act_ref, z_ref, bl_ref, al_ref, s0_ref, alog_ref, dtb_ref, gw_ref,
              o_ref, snew_ref, s_ref, u_ref, wq_ref, qk_ref, kdec_ref, *, tb, c, n_valid):
    t = pl.program_id(1)

    @pl.when(t == 0)
    def _():
        s_ref[...] = s0_ref[...]

    beta_all = jax.nn.sigmoid(bl_ref[...])
    a_in = al_ref[...] + dtb_ref[...]
    softplus = jnp.maximum(a_in, 0.0) + jnp.log1p(jnp.exp(-jnp.abs(a_in)))
    g_all = -jnp.exp(alog_ref[...]) * softplus
    if n_valid < tb:
        live = lax.broadcasted_iota(jnp.int32, (tb, GATE_W), 0) < n_valid
        beta_all = jnp.where(live, beta_all, 0.0)
        g_all = jnp.where(live, g_all, 0.0)

    row = lax.broadcasted_iota(jnp.int32, (c, c), 0)
    col = lax.broadcasted_iota(jnp.int32, (c, c), 1)
    causal = row >= col
    strict = row > col
    tril = jnp.where(causal, 1.0, 0.0).astype(BF16)
    gw = gw_ref[...]
    n_chunks = tb // c

    g_tots = []
    lmats, rhss = [], []
    for ci in range(n_chunks):
        r0 = ci * c
        beta = beta_all[r0:r0 + c, :]
        g_cum = _chunk_cumsum(g_all[r0:r0 + c, :], tril)
        g_cum_t = g_cum.T
        g_last = g_cum[c - 1:c, :]
        e_g = jnp.exp(g_cum)
        e_gl = jnp.exp(g_last - g_cum)
        g_tots.append(jnp.exp(g_last))
        for h in range(A_HEADS):
            i = ci * A_HEADS + h
            q = act_ref[r0:r0 + c, h * A_DK:(h + 1) * A_DK]
            k = act_ref[r0:r0 + c, A_QK + h * A_DK:A_QK + (h + 1) * A_DK]
            v = act_ref[r0:r0 + c, 2 * A_QK + h * A_DV:2 * A_QK + (h + 1) * A_DV]
            q = q * lax.rsqrt(jnp.sum(q * q, axis=-1, keepdims=True) + EPS) * (A_DK ** -0.5)
            k = k * lax.rsqrt(jnp.sum(k * k, axis=-1, keepdims=True) + EPS)
            b_col = beta[:, h:h + 1]
            eg_col = e_g[:, h:h + 1]
            decay = jnp.exp(jnp.where(causal, g_cum[:, h:h + 1] - g_cum_t[h:h + 1, :], -jnp.inf))
            k16 = k.astype(BF16)
            qk_kk = lax.dot_general(jnp.concatenate([q.astype(BF16), k16], axis=0), k16,
                                    (((1,), (1,)), ((), ())), preferred_element_type=F32)
            qk_ref[i] = (qk_kk[0:c, :] * decay).astype(BF16)
            lmats.append(jnp.where(strict, qk_kk[c:2 * c, :] * decay * b_col, 0.0))
            rhss.append(jnp.concatenate([v * b_col, k * (b_col * eg_col)], axis=1))
            wq_ref[i, c:2 * c, :] = (q * eg_col).astype(BF16)
            kdec_ref[i] = (k * e_gl[:, h:h + 1]).astype(BF16)
    tinvs = _unit_lower_inverses_minus_eye(lmats, c)
    for i, (tinv, rhs) in enumerate(zip(tinvs, rhss)):
        sol = rhs + _mm(tinv, rhs)
        u_ref[i] = sol[:, 0:A_DV]
        wq_ref[i, 0:c, :] = sol[:, A_DV:A_DV + A_DK].astype(BF16)

    for ci in range(n_chunks):
        r0 = ci * c
        idx = [ci * A_HEADS + h for h in range(A_HEADS)]
        s_old = [s_ref[h] for h in range(A_HEADS)]
        wq_s = [jnp.dot(wq_ref[i], s.astype(BF16), preferred_element_type=F32) for i, s in zip(idx, s_old)]
        v_new = [u_ref[i] - x[0:c, :] for i, x in zip(idx, wq_s)]
        v16 = [x.astype(BF16) for x in v_new]
        for h in range(A_HEADS):
            s_ref[h] = s_old[h] * g_tots[ci][:, h:h + 1] + lax.dot_general(
                kdec_ref[idx[h]], v16[h], (((0,), (0,)), ((), ())), preferred_element_type=F32)
        for h in range(A_HEADS):
            o = wq_s[h][c:2 * c, :] + jnp.dot(qk_ref[idx[h]], v16[h], preferred_element_type=F32)
            o = o * lax.rsqrt(jnp.mean(o * o, axis=-1, keepdims=True) + EPS) * gw
            o_ref[r0:r0 + c, h * A_DV:(h + 1) * A_DV] = o * _silu(z_ref[r0:r0 + c, h * A_DV:(h + 1) * A_DV])

    @pl.when(t == pl.num_programs(1) - 1)
    def _():
        snew_ref[...] = s_ref[...]


def _gdn(act, z, bl, al, s0, alog, dtb, gw, *, tb, c, n_valid):
    b, l, _ = act.shape
    nt = l // tb
    assert n_valid == tb or nt == 1
    n_items = (tb // c) * A_HEADS
    blk = lambda width: pl.BlockSpec((None, tb, width), lambda i, j: (i, j, 0))
    state = pl.BlockSpec((None, A_HEADS, A_DK, A_DV), lambda i, j: (i, 0, 0, 0))
    lane_row = pl.BlockSpec((1, LANES), lambda i, j: (0, 0))
    return pl.pallas_call(
        functools.partial(_gdn_body, tb=tb, c=c, n_valid=n_valid),
        grid=(b, nt),
        in_specs=[blk(A_QKV), blk(A_VW), blk(GATE_W), blk(GATE_W), state, lane_row, lane_row, lane_row],
        out_specs=[blk(A_VW), state],
        out_shape=[jax.ShapeDtypeStruct((b, l, A_VW), F32),
                   jax.ShapeDtypeStruct((b, A_HEADS, A_DK, A_DV), F32)],
        scratch_shapes=[pltpu.VMEM((A_HEADS, A_DK, A_DV), F32),
                        pltpu.VMEM((n_items, c, A_DV), F32),
                        pltpu.VMEM((n_items, 2 * c, A_DK), BF16),
                        pltpu.VMEM((n_items, c, c), BF16),
                        pltpu.VMEM((n_items, c, A_DK), BF16)],
        compiler_params=_params(("parallel", "arbitrary")),
        name="gdn",
    )(act, z, bl, al, s0, alog, dtb, gw)


def _rope_table_body(inv_ref, ma_ref, mb_ref, cos_ref, sa_ref, sb_ref, *, tm, pos0):
    pos = pos0 + pl.program_id(0) * tm + lax.broadcasted_iota(jnp.int32, (tm, LANES), 0)
    ang = pos.astype(F32) * inv_ref[...]
    sin = jnp.sin(ang)
    cos_ref[...] = jnp.cos(ang)
    sa_ref[...] = sin * ma_ref[...]
    sb_ref[...] = -sin * mb_ref[...]


def _rope_tables(inv_row, ma, mb, *, length, pos0, tm):
    row = pl.BlockSpec((1, LANES), lambda i: (0, 0))
    out = pl.BlockSpec((tm, LANES), lambda i: (i, 0))
    return pl.pallas_call(
        functools.partial(_rope_table_body, tm=tm, pos0=pos0),
        grid=(length // tm,),
        in_specs=[row, row, row],
        out_specs=[out, out, out],
        out_shape=[jax.ShapeDtypeStruct((length, LANES), F32)] * 3,
        compiler_params=_params(("parallel",)),
        name="rope_tables",
    )(inv_row, ma, mb)


SWA_LOCKSTEP = 4


def _swa_body(qb_ref, kb_ref, vb_ref, cos_ref, sa_ref, sb_ref, kc_ref, vc_ref, sinks_ref,
              o_ref, newk_ref, newv_ref, kx_ref, vx_ref, qs_ref, kvar_ref, vvar_ref,
              *, tb, c, mask_start):
    t = pl.program_id(1)
    half = B_HD
    rot = ROPE_DIM // 2
    w = WINDOW + c

    @pl.when(t == 0)
    def _():
        kx_ref[0:WINDOW, :] = kc_ref[...]
        vx_ref[0:WINDOW, :] = vc_ref[...]

    cos = cos_ref[...]
    sa = sa_ref[...]
    sb = sb_ref[...]

    def rope(x):
        return x * cos + pltpu.roll(x, rot, 1) * sa + pltpu.roll(x, LANES - rot, 1) * sb

    kx_ref[WINDOW:WINDOW + tb, :] = rope(kb_ref[...])
    vx_ref[WINDOW:WINDOW + tb, :] = vb_ref[...]
    for s in range(B_QW // LANES):
        qs_ref[:, s * LANES:(s + 1) * LANES] = (
            rope(qb_ref[:, s * LANES:(s + 1) * LANES]) * (B_HD ** -0.5)).astype(BF16)

    lo = lax.broadcasted_iota(jnp.int32, (WINDOW + tb, LANES), 1) < half
    for src, dst in ((kx_ref, kvar_ref), (vx_ref, vvar_ref)):
        full = src[...]
        swapped = pltpu.roll(full, half, 1)
        dst[0] = jnp.where(lo, full, 0.0).astype(BF16)
        dst[1] = jnp.where(lo, 0.0, swapped).astype(BF16)
        dst[2] = jnp.where(lo, swapped, 0.0).astype(BF16)
        dst[3] = jnp.where(lo, 0.0, full).astype(BF16)

    first_rows = lax.broadcasted_iota(jnp.int32, (2 * c, 1), 0) < c
    kcol = lax.broadcasted_iota(jnp.int32, (2 * c, w), 1)

    sks = [jnp.where(first_rows, sinks_ref[4 * g + hh], sinks_ref[4 * g + 2 + hh])
           for g in range(B_KV_HEADS) for hh in range(2)]
    n_chunks = tb // c
    for i0 in range(0, n_chunks, SWA_LOCKSTEP):
        items = [(i, g, hh) for i in range(i0, min(i0 + SWA_LOCKSTEP, n_chunks))
                 for g in range(B_KV_HEADS) for hh in range(2)]
        lhs = {(i, g): jnp.concatenate([qs_ref[i * c:(i + 1) * c, (2 * g) * LANES:(2 * g + 1) * LANES],
                                        qs_ref[i * c:(i + 1) * c, (2 * g + 1) * LANES:(2 * g + 2) * LANES]], axis=0)
               for i, g, hh in items if hh == 0}
        s = [lax.dot_general(lhs[i, g], kvar_ref[2 * g + hh, i * c:i * c + w, :],
                             (((1,), (1,)), ((), ())), preferred_element_type=F32) for i, g, hh in items]
        if mask_start:
            s = [jnp.where(t * tb + i * c - WINDOW + kcol >= 0, si, -jnp.inf) for si, (i, g, hh) in zip(s, items)]
        m = [jnp.maximum(jnp.max(si, axis=-1, keepdims=True), sks[2 * g + hh]) for si, (i, g, hh) in zip(s, items)]
        p = [jnp.exp(si - mi) for si, mi in zip(s, m)]
        den = [jnp.sum(pi, axis=-1, keepdims=True) + jnp.exp(sks[2 * g + hh] - mi)
               for pi, mi, (i, g, hh) in zip(p, m, items)]
        pv = [jnp.dot(pi.astype(BF16), vvar_ref[2 * g + hh, i * c:i * c + w, :], preferred_element_type=F32) / di
              for pi, di, (i, g, hh) in zip(p, den, items)]
        for j in range(0, len(items), 2):
            i, g, _ = items[j]
            o = pv[j] + pv[j + 1]
            o_ref[i * c:(i + 1) * c, (2 * g) * LANES:(2 * g + 1) * LANES] = o[0:c, :]
            o_ref[i * c:(i + 1) * c, (2 * g + 1) * LANES:(2 * g + 2) * LANES] = o[c:2 * c, :]

    @pl.when(t == pl.num_programs(1) - 1)
    def _():
        newk_ref[...] = kx_ref[tb:tb + WINDOW, :]
        newv_ref[...] = vx_ref[tb:tb + WINDOW, :]

    if tb >= WINDOW:
        kx_ref[0:WINDOW, :] = kx_ref[tb:tb + WINDOW, :]
        vx_ref[0:WINDOW, :] = vx_ref[tb:tb + WINDOW, :]


def _swa(qb, kb, vb, cos, sa, sb, kc, vc, sinks, *, tb, c, mask_start):
    b, l, _ = qb.shape
    nt = l // tb
    assert tb >= WINDOW or nt == 1
    blk = lambda width: pl.BlockSpec((None, tb, width), lambda i, j: (i, j, 0))
    tab = pl.BlockSpec((tb, LANES), lambda i, j: (j, 0))
    cache = pl.BlockSpec((None, WINDOW, B_KW), lambda i, j: (i, 0, 0))
    return pl.pallas_call(
        functools.partial(_swa_body, tb=tb, c=c, mask_start=mask_start),
        grid=(b, nt),
        in_specs=[blk(B_QW), blk(B_KW), blk(B_KW), tab, tab, tab, cache, cache,
                  pl.BlockSpec(memory_space=pltpu.SMEM)],
        out_specs=[blk(B_QW), cache, cache],
        out_shape=[jax.ShapeDtypeStruct((b, l, B_QW), F32),
                   jax.ShapeDtypeStruct((b, WINDOW, B_KW), F32),
                   jax.ShapeDtypeStruct((b, WINDOW, B_KW), F32)],
        scratch_shapes=[pltpu.VMEM((WINDOW + tb, B_KW), F32),
                        pltpu.VMEM((WINDOW + tb, B_KW), F32),
                        pltpu.VMEM((tb, B_QW), BF16),
                        pltpu.VMEM((4, WINDOW + tb, LANES), BF16),
                        pltpu.VMEM((4, WINDOW + tb, LANES), BF16)],
        compiler_params=_params(("parallel", "arbitrary")),
        name="swa",
    )(qb, kb, vb, cos, sa, sb, kc, vc, sinks)


def _pack_w_in_tail(w):
    offs = np.cumsum([O_TAIL, A_HEADS, A_HEADS, B_QW, B_KW, B_KW]).tolist()
    bg, ag, qb, kb, vb = [w[..., offs[i]:offs[i + 1]] for i in range(5)]
    pad = lambda g: jnp.pad(g, ((0, 0), (0, 0), (0, GATE_W - g.shape[-1])))
    return jnp.concatenate([pad(bg), pad(ag), qb, kb, vb], axis=-1).astype(BF16)


def _pad_lanes(v):
    return jnp.pad(v, ((0, 0), (0, GATE_W - v.shape[-1])))[:, None, :]


def _trunk(x, rope_tabs, conv_bufs, s0s, k_caches, v_caches, wts, *, batch, length, tm, proj_tm,
           gdn_tb, gdn_c, swa_tb, swa_c, mask_start):
    (norm_ff1, ff1_in, ff1_out, norm_mix, w_in, w_in_tail, conv_w, alog, dtb, gnorm, sinks, w_out,
     norm_ff2, ff2_in, ff2_out, norm_final) = wts
    cos, sa, sb = rope_tabs
    lpad = -(-length // gdn_tb) * gdn_tb
    bufs, states, ks, vs = [], [], [], []

    def pad_rows(a):
        return a if lpad == length else jnp.pad(a, ((0, 0), (0, lpad - length), (0, 0)))

    for l in range(DEPTH):
        x = _ffn(x, norm_ff1, ff1_in, ff1_out, layer=l, tm=tm)
        act, z, bg, ag, qb, kb, vb, nbuf = _proj_in(x.reshape(batch, length, D_MODEL), norm_mix, w_in, w_in_tail,
                                                    conv_bufs[l], conv_w, layer=l, tm=proj_tm)
        o_a, s_new = _gdn(pad_rows(act), pad_rows(z), pad_rows(bg), pad_rows(ag), s0s[l], alog[l], dtb[l], gnorm[l],
                          tb=gdn_tb, c=gdn_c, n_valid=min(length, gdn_tb))
        o_b, nk, nv = _swa(qb, kb, vb, cos, sa, sb, k_caches[l], v_caches[l], sinks[l],
                           tb=swa_tb, c=swa_c, mask_start=mask_start)
        o_a = o_a[:, :length].reshape(batch * length, A_VW)
        x = _mix_out_ffn(x, o_a, o_b.reshape(batch * length, B_QW), w_out, norm_ff2, ff2_in, ff2_out,
                         norm_final if l == DEPTH - 1 else None, layer=l, tm=tm)
        bufs.append(nbuf)
        states.append(s_new)
        ks.append(nk.reshape(batch, WINDOW, B_KV_HEADS, B_HD))
        vs.append(nv.reshape(batch, WINDOW, B_KV_HEADS, B_HD))
    y = x.reshape(batch, length, D_MODEL)
    return y, jnp.stack(bufs), jnp.stack(states), jnp.stack(ks), jnp.stack(vs)


def kernel(x_prompt, x_sample, cache_conv, state_delta, cache_k, cache_v, norm_ff1, ff1_w_in, ff1_w_out, norm_mix, w_mix_in, conv_w, a_log, dt_bias, gnorm_w, sinks, w_mix_out, norm_ff2, ff2_w_in, ff2_w_out, norm_final):
    bp, lp, _ = x_prompt.shape
    bs, ls, _ = x_sample.shape
    rows = cache_k.shape[2]
    assert rows == WINDOW

    wts = (norm_ff1[:, None, :], ff1_w_in.astype(BF16), ff1_w_out.astype(BF16), norm_mix[:, None, :],
           w_mix_in.astype(BF16), _pack_w_in_tail(w_mix_in), conv_w, _pad_lanes(a_log), _pad_lanes(dt_bias), gnorm_w[:, None, :], sinks,
           w_mix_out.astype(BF16), norm_ff2[:, None, :], ff2_w_in.astype(BF16), ff2_w_out.astype(BF16),
           norm_final[None, :])

    inv = jnp.power(ROPE_THETA, -jnp.arange(0, ROPE_DIM, 2, dtype=F32) / ROPE_DIM)
    rot = ROPE_DIM // 2
    head_row = jnp.concatenate([inv, inv, jnp.zeros((B_HD - ROPE_DIM,), F32)])
    inv_row = jnp.tile(head_row, LANES // B_HD)[None, :]
    d = np.arange(LANES) % B_HD
    ma = jnp.asarray(((d >= rot) & (d < ROPE_DIM)).astype(np.float32))[None, :]
    mb = jnp.asarray((d < rot).astype(np.float32))[None, :]

    zero_buf = jnp.zeros((DEPTH, bp, CONV_W - 1, A_QKV), F32)
    zero_state = jnp.zeros((DEPTH, bp, A_HEADS, A_DK, A_DV), F32)
    zero_kv = jnp.zeros((DEPTH, bp, WINDOW, B_KW), F32)
    tabs_p = _rope_tables(inv_row, ma, mb, length=lp, pos0=0, tm=512)
    y_p, conv_p, delta_p, k_p, v_p = _trunk(
        x_prompt.reshape(bp * lp, D_MODEL), tabs_p, zero_buf, zero_state, zero_kv, zero_kv, wts,
        batch=bp, length=lp, tm=512, proj_tm=512, gdn_tb=512, gdn_c=128, swa_tb=512, swa_c=64, mask_start=True)

    tabs_s = _rope_tables(inv_row, ma, mb, length=ls, pos0=PAST_LEN, tm=ls)
    y_s, conv_s, delta_s, k_s, v_s = _trunk(
        x_sample.reshape(bs * ls, D_MODEL), tabs_s, cache_conv, state_delta,
        cache_k.reshape(DEPTH, bs, WINDOW, B_KW), cache_v.reshape(DEPTH, bs, WINDOW, B_KW), wts,
        batch=bs, length=ls, tm=bs * ls, proj_tm=ls, gdn_tb=128, gdn_c=128, swa_tb=ls, swa_c=ls, mask_start=False)
    return (y_p, y_s, conv_p, delta_p, k_p, v_p, conv_s, delta_s, k_s, v_s)
```

```python
import functools

import numpy as np
import jax
import jax.numpy as jnp
from jax import lax
from jax.experimental import pallas as pl
from jax.experimental.pallas import tpu as pltpu

F32 = jnp.float32
BF16 = jnp.bfloat16

D_MODEL = 1024
DEPTH = 4
PAST_LEN = 4096
EPS = 1e-6
A_HEADS = 4
A_DK = 128
A_DV = 128
A_QK = A_HEADS * A_DK
A_VW = A_HEADS * A_DV
A_QKV = 2 * A_QK + A_VW
CONV_W = 4
B_HEADS = 8
B_KV_HEADS = 2
B_HD = 64
B_QW = B_HEADS * B_HD
B_KW = B_KV_HEADS * B_HD
WINDOW = 128
ROPE_DIM = B_HD // 4
ROPE_THETA = 500000.0
D_MIX = A_VW + B_QW
D_FF = 2816

LANES = 128
SUBLANES = 8
GATE_W = LANES
VMEM_LIMIT = 56 * 1024 * 1024


def _params(sem, vmem=VMEM_LIMIT):
    return pltpu.CompilerParams(dimension_semantics=sem, vmem_limit_bytes=vmem)


def _resident(shape, layer=None):
    nd = len(shape)
    if layer is None:
        return pl.BlockSpec(shape, lambda *_: (0,) * nd, pipeline_mode=pl.Buffered(1))
    return pl.BlockSpec((None,) + tuple(shape), lambda *_: (layer,) + (0,) * nd, pipeline_mode=pl.Buffered(1))


def _rms(x, w):
    return x * lax.rsqrt(jnp.mean(x * x, axis=-1, keepdims=True) + EPS) * w


def _silu(x):
    return x * jax.nn.sigmoid(x)


def _mm(a, b):
    return jnp.dot(a.astype(BF16), b.astype(BF16), preferred_element_type=F32)


def _swiglu_residual(x, nw_ref, win_ref, wout_ref, fc):
    h = _rms(x, nw_ref[...]).astype(BF16)
    acc = None
    for c in range(0, D_FF, fc):
        gate = jnp.dot(h, win_ref[:, c:c + fc], preferred_element_type=F32)
        up = jnp.dot(h, win_ref[:, D_FF + c:D_FF + c + fc], preferred_element_type=F32)
        act = (_silu(gate) * up).astype(BF16)
        part = jnp.dot(act, wout_ref[c:c + fc, :], preferred_element_type=F32)
        acc = part if acc is None else acc + part
    return x + 0.5 * acc


FFN_ROWS = 512


def _row_groups(n):
    return [(r, min(FFN_ROWS, n - r)) for r in range(0, n, FFN_ROWS)]


def _ffn_body(x_ref, nw_ref, win_ref, wout_ref, o_ref, *, fc):
    for r, n in _row_groups(x_ref.shape[0]):
        o_ref[r:r + n, :] = _swiglu_residual(x_ref[r:r + n, :], nw_ref, win_ref, wout_ref, fc)


def _ffn(x, nw, w_in, w_out, *, layer, tm, fc=2816):
    m = x.shape[0]
    return pl.pallas_call(
        functools.partial(_ffn_body, fc=fc),
        grid=(m // tm,),
        in_specs=[pl.BlockSpec((tm, D_MODEL), lambda i: (i, 0)),
                  _resident((1, D_MODEL), layer),
                  _resident((D_MODEL, 2 * D_FF), layer),
                  _resident((D_FF, D_MODEL), layer)],
        out_specs=pl.BlockSpec((tm, D_MODEL), lambda i: (i, 0)),
        out_shape=jax.ShapeDtypeStruct((m, D_MODEL), F32),
        compiler_params=_params(("parallel",)),
        name="ffn",
    )(x, nw, w_in, w_out)


def _mix_out_ffn_body(x_ref, oa_ref, ob_ref, wo_ref, nw_ref, win_ref, wout_ref, *rest, fc, final):
    o_ref = rest[-1]
    for r, n in _row_groups(x_ref.shape[0]):
        x = (x_ref[r:r + n, :]
             + jnp.dot(oa_ref[r:r + n, :].astype(BF16), wo_ref[0:A_VW, :], preferred_element_type=F32)
             + jnp.dot(ob_ref[r:r + n, :].astype(BF16), wo_ref[A_VW:D_MIX, :], preferred_element_type=F32))
        y = _swiglu_residual(x, nw_ref, win_ref, wout_ref, fc)
        o_ref[r:r + n, :] = _rms(y, rest[0][...]) if final else y


def _mix_out_ffn(x, oa, ob, wo, nw, w_in, w_out, norm_final=None, *, layer, tm, fc=2816):
    m = x.shape[0]
    final = norm_final is not None
    row = lambda width: pl.BlockSpec((tm, width), lambda i: (i, 0))
    in_specs = [row(D_MODEL), row(A_VW), row(B_QW), _resident((D_MIX, D_MODEL), layer),
                _resident((1, D_MODEL), layer), _resident((D_MODEL, 2 * D_FF), layer),
                _resident((D_FF, D_MODEL), layer)]
    args = [x, oa, ob, wo, nw, w_in, w_out]
    if final:
        in_specs.append(_resident((1, D_MODEL)))
        args.append(norm_final)
    return pl.pallas_call(
        functools.partial(_mix_out_ffn_body, fc=fc, final=final),
        grid=(m // tm,),
        in_specs=in_specs,
        out_specs=row(D_MODEL),
        out_shape=jax.ShapeDtypeStruct((m, D_MODEL), F32),
        compiler_params=_params(("parallel",)),
        name="mix_out_ffn",
    )(*args)


O_TAIL = A_QKV + A_VW
_TAIL_GROUPS = (GATE_W, GATE_W, B_QW, B_KW, B_KW)
D_TAIL = sum(_TAIL_GROUPS)
HALO = CONV_W - 1
HALO_BASE = SUBLANES


def _proj_in_body(x_ref, nw_ref, w_ref, wt_ref, cbuf_ref, cw_ref, act_ref, z_ref, *rest, tm):
    tail_refs, (nbuf_ref, xp_ref) = rest[:-2], rest[-2:]
    j = pl.program_id(1)

    @pl.when(j == 0)
    def _():
        xp_ref[HALO_BASE - HALO:HALO_BASE, :] = cbuf_ref[...]

    h = _rms(x_ref[...], nw_ref[...]).astype(BF16)
    qkv = jnp.dot(h, w_ref[:, 0:A_QKV], preferred_element_type=F32)
    xp_ref[HALO_BASE:HALO_BASE + tm, :] = qkv
    z_ref[...] = jnp.dot(h, w_ref[:, A_QKV:O_TAIL], preferred_element_type=F32)
    off = 0
    for width, o_ref in zip(_TAIL_GROUPS, tail_refs):
        o_ref[...] = jnp.dot(h, wt_ref[:, off:off + width], preferred_element_type=F32)
        off += width

    cw = cw_ref[...]
    conv = qkv * cw[0:1, :]
    for t in range(1, CONV_W):
        conv = pltpu.roll(conv, 1, 0) + qkv * cw[t:t + 1, :]
    act_ref[...] = _silu(conv)
    lo = HALO_BASE - HALO
    head = xp_ref[lo:lo + SUBLANES, :] * cw[0:1, :]
    for t in range(1, CONV_W):
        head = head + xp_ref[lo + t:lo + t + SUBLANES, :] * cw[t:t + 1, :]
    act_ref[0:SUBLANES, :] = _silu(head)

    @pl.when(j == pl.num_programs(1) - 1)
    def _():
        nbuf_ref[...] = xp_ref[HALO_BASE + tm - HALO:HALO_BASE + tm, :]

    xp_ref[HALO_BASE - HALO:HALO_BASE, :] = xp_ref[HALO_BASE + tm - HALO:HALO_BASE + tm, :]


def _proj_in(x, nw, w, w_tail, cbuf, cw, *, layer, tm):
    b, l, _ = x.shape
    blk = lambda width: pl.BlockSpec((None, tm, width), lambda i, j: (i, j, 0))
    buf = pl.BlockSpec((None, HALO, A_QKV), lambda i, j: (i, 0, 0))
    widths = [A_QKV, A_VW] + list(_TAIL_GROUPS)
    return pl.pallas_call(
        functools.partial(_proj_in_body, tm=tm),
        grid=(b, l // tm),
        in_specs=[blk(D_MODEL), _resident((1, D_MODEL), layer), _resident((D_MODEL, O_TAIL), layer),
                  _resident((D_MODEL, D_TAIL), layer), buf, _resident((CONV_W, A_QKV), layer)],
        out_specs=[blk(width) for width in widths] + [buf],
        out_shape=[jax.ShapeDtypeStruct((b, l, width), F32) for width in widths]
                  + [jax.ShapeDtypeStruct((b, HALO, A_QKV), F32)],
        scratch_shapes=[pltpu.VMEM((HALO_BASE + tm, A_QKV), F32)],
        compiler_params=_params(("parallel", "arbitrary")),
        name="proj_in",
    )(x, nw, w, w_tail, cbuf, cw)


def _chunk_cumsum(g, tril):
    g1 = g.astype(BF16)
    r1 = g - g1.astype(F32)
    g2 = r1.astype(BF16)
    g3 = (r1 - g2.astype(F32)).astype(BF16)
    parts = jnp.dot(tril, jnp.concatenate([g1, g2, g3], axis=1), preferred_element_type=F32)
    return parts[:, 0:LANES] + parts[:, LANES:2 * LANES] + parts[:, 2 * LANES:3 * LANES]


INV_BLOCK = 32


def _unit_lower_inverses_minus_eye(ls, c):
    bs = min(INV_BLOCK, c)
    row = lax.broadcasted_iota(jnp.int32, (c, c), 0)
    col = lax.broadcasted_iota(jnp.int32, (c, c), 1)
    same = lambda b: (row // b) == (col // b)
    diag = [jnp.where(same(bs), l, 0.0) for l in ls] if bs < c else ls
    p = [-d for d in diag]
    m = [_mm(d, d) for d in diag]
    for _ in range(int(np.log2(bs)) - 2):
        pm = [_mm(pi, mi) for pi, mi in zip(p, m)]
        p = [pi + mi + pmi for pi, mi, pmi in zip(p, m, pm)]
        m = [_mm(mi, mi) for mi in m]
    pm = [_mm(pi, mi) for pi, mi in zip(p, m)]
    p = [pi + mi + pmi for pi, mi, pmi in zip(p, m, pm)]
    b = bs
    while b < c:
        sel = same(2 * b) & jnp.logical_not(same(b))
        off = [jnp.where(sel, l, 0.0) for l in ls]
        x = [oi + _mm(pi, oi) for pi, oi in zip(p, off)]
        p = [pi - (xi + _mm(xi, pi)) for pi, xi in zip(p, x)]
        b *= 2
    return p


def _gdn_body(act_ref, z_ref, bl_ref, al_ref, s0_ref, alog_ref, dtb_ref, gw_ref,
              o_ref, snew_ref, s_ref, u_ref, wq_ref, qk_ref, kdec_ref, *, tb, c, n_valid):
    t = pl.program_id(1)

    @pl.when(t == 0)
    def _():
        s_ref[...] = s0_ref[...]

    beta_all = jax.nn.sigmoid(bl_ref[...])
    a_in = al_ref[...] + dtb_ref[...]
    softplus = jnp.maximum(a_in, 0.0) + jnp.log1p(jnp.exp(-jnp.abs(a_in)))
    g_all = -jnp.exp(alog_ref[...]) * softplus
    if n_valid < tb:
        live = lax.broadcasted_iota(jnp.int32, (tb, GATE_W), 0) < n_valid
        beta_all = jnp.where(live, beta_all, 0.0)
        g_all = jnp.where(live, g_all, 0.0)

    row = lax.broadcasted_iota(jnp.int32, (c, c), 0)
    col = lax.broadcasted_iota(jnp.int32, (c, c), 1)
    causal = row >= col
    strict = row > col
    tril = jnp.where(causal, 1.0, 0.0).astype(BF16)
    gw = gw_ref[...]
    n_chunks = tb // c

    g_tots = []
    lmats, rhss = [], []
    for ci in range(n_chunks):
        r0 = ci * c
        beta = beta_all[r0:r0 + c, :]
        g_cum = _chunk_cumsum(g_all[r0:r0 + c, :], tril)
        g_cum_t = g_cum.T
        g_last = g_cum[c - 1:c, :]
        e_g = jnp.exp(g_cum)
        e_gl = jnp.exp(g_last - g_cum)
        g_tots.append(jnp.exp(g_last))
        for h in range(A_HEADS):
            i = ci * A_HEADS + h
            q = act_ref[r0:r0 + c, h * A_DK:(h + 1) * A_DK]
            k = act_ref[r0:r0 + c, A_QK + h * A_DK:A_QK + (h + 1) * A_DK]
            v = act_ref[r0:r0 + c, 2 * A_QK + h * A_DV:2 * A_QK + (h + 1) * A_DV]
            q = q * lax.rsqrt(jnp.sum(q * q, axis=-1, keepdims=True) + EPS) * (A_DK ** -0.5)
            k = k * lax.rsqrt(jnp.sum(k * k, axis=-1, keepdims=True) + EPS)
            b_col = beta[:, h:h + 1]
            eg_col = e_g[:, h:h + 1]
            decay = jnp.exp(jnp.where(causal, g_cum[:, h:h + 1] - g_cum_t[h:h + 1, :], -jnp.inf))
            k16 = k.astype(BF16)
            qk_kk = lax.dot_general(jnp.concatenate([q.astype(BF16), k16], axis=0), k16,
                                    (((1,), (1,)), ((), ())), preferred_element_type=F32)
            qk_ref[i] = (qk_kk[0:c, :] * decay).astype(BF16)
            lmats.append(jnp.where(strict, qk_kk[c:2 * c, :] * decay * b_col, 0.0))
            rhss.append(jnp.concatenate([v * b_col, k * (b_col * eg_col)], axis=1))
            wq_ref[i, c:2 * c, :] = (q * eg_col).astype(BF16)
            kdec_ref[i] = (k * e_gl[:, h:h + 1]).astype(BF16)
    tinvs = _unit_lower_inverses_minus_eye(lmats, c)
    for i, (tinv, rhs) in enumerate(zip(tinvs, rhss)):
        sol = rhs + _mm(tinv, rhs)
        u_ref[i] = sol[:, 0:A_DV]
        wq_ref[i, 0:c, :] = sol[:, A_DV:A_DV + A_DK].astype(BF16)

    for ci in range(n_chunks):
        r0 = ci * c
        idx = [ci * A_HEADS + h for h in range(A_HEADS)]
        s_old = [s_ref[h] for h in range(A_HEADS)]
        wq_s = [jnp.dot(wq_ref[i], s.astype(BF16), preferred_element_type=F32) for i, s in zip(idx, s_old)]
        v_new = [u_ref[i] - x[0:c, :] for i, x in zip(idx, wq_s)]
        v16 = [x.astype(BF16) for x in v_new]
        for h in range(A_HEADS):
            s_ref[h] = s_old[h] * g_tots[ci][:, h:h + 1] + lax.dot_general(
                kdec_ref[idx[h]], v16[h], (((0,), (0,)), ((), ())), preferred_element_type=F32)
        for h in range(A_HEADS):
            o = wq_s[h][c:2 * c, :] + jnp.dot(qk_ref[idx[h]], v16[h], preferred_element_type=F32)
            o = o * lax.rsqrt(jnp.mean(o * o, axis=-1, keepdims=True) + EPS) * gw
            o_ref[r0:r0 + c, h * A_DV:(h + 1) * A_DV] = o * _silu(z_ref[r0:r0 + c, h * A_DV:(h + 1) * A_DV])

    @pl.when(t == pl.num_programs(1) - 1)
    def _():
        snew_ref[...] = s_ref[...]


def _gdn(act, z, bl, al, s0, alog, dtb, gw, *, tb, c, n_valid):
    b, l, _ = act.shape
    nt = l // tb
    assert n_valid == tb or nt == 1
    n_items = (tb // c) * A_HEADS
    blk = lambda width: pl.BlockSpec((None, tb, width), lambda i, j: (i, j, 0))
    state = pl.BlockSpec((None, A_HEADS, A_DK, A_DV), lambda i, j: (i, 0, 0, 0))
    lane_row = pl.BlockSpec((1, LANES), lambda i, j: (0, 0))
    return pl.pallas_call(
        functools.partial(_gdn_body, tb=tb, c=c, n_valid=n_valid),
        grid=(b, nt),
        in_specs=[blk(A_QKV), blk(A_VW), blk(GATE_W), blk(GATE_W), state, lane_row, lane_row, lane_row],
        out_specs=[blk(A_VW), state],
        out_shape=[jax.ShapeDtypeStruct((b, l, A_VW), F32),
                   jax.ShapeDtypeStruct((b, A_HEADS, A_DK, A_DV), F32)],
        scratch_shapes=[pltpu.VMEM((A_HEADS, A_DK, A_DV), F32),
                        pltpu.VMEM((n_items, c, A_DV), F32),
                        pltpu.VMEM((n_items, 2 * c, A_DK), BF16),
                        pltpu.VMEM((n_items, c, c), BF16),
                        pltpu.VMEM((n_items, c, A_DK), BF16)],
        compiler_params=_params(("parallel", "arbitrary")),
        name="gdn",
    )(act, z, bl, al, s0, alog, dtb, gw)


def _rope_table_body(inv_ref, ma_ref, mb_ref, cos_ref, sa_ref, sb_ref, *, tm, pos0):
    pos = pos0 + pl.program_id(0) * tm + lax.broadcasted_iota(jnp.int32, (tm, LANES), 0)
    ang = pos.astype(F32) * inv_ref[...]
    sin = jnp.sin(ang)
    cos_ref[...] = jnp.cos(ang)
    sa_ref[...] = sin * ma_ref[...]
    sb_ref[...] = -sin * mb_ref[...]


def _rope_tables(inv_row, ma, mb, *, length, pos0, tm):
    row = pl.BlockSpec((1, LANES), lambda i: (0, 0))
    out = pl.BlockSpec((tm, LANES), lambda i: (i, 0))
    return pl.pallas_call(
        functools.partial(_rope_table_body, tm=tm, pos0=pos0),
        grid=(length // tm,),
        in_specs=[row, row, row],
        out_specs=[out, out, out],
        out_shape=[jax.ShapeDtypeStruct((length, LANES), F32)] * 3,
        compiler_params=_params(("parallel",)),
        name="rope_tables",
    )(inv_row, ma, mb)


SWA_LOCKSTEP = 4


def _swa_body(qb_ref, kb_ref, vb_ref, cos_ref, sa_ref, sb_ref, kc_ref, vc_ref, sinks_ref,
              o_ref, newk_ref, newv_ref, kx_ref, vx_ref, qs_ref, kvar_ref, vvar_ref,
              *, tb, c, mask_start):
    t = pl.program_id(1)
    half = B_HD
    rot = ROPE_DIM // 2
    w = WINDOW + c

    @pl.when(t == 0)
    def _():
        kx_ref[0:WINDOW, :] = kc_ref[...]
        vx_ref[0:WINDOW, :] = vc_ref[...]

    cos = cos_ref[...]
    sa = sa_ref[...]
    sb = sb_ref[...]

    def rope(x):
        return x * cos + pltpu.roll(x, rot, 1) * sa + pltpu.roll(x, LANES - rot, 1) * sb

    kx_ref[WINDOW:WINDOW + tb, :] = rope(kb_ref[...])
    vx_ref[WINDOW:WINDOW + tb, :] = vb_ref[...]
    for s in range(B_QW // LANES):
        qs_ref[:, s * LANES:(s + 1) * LANES] = (
            rope(qb_ref[:, s * LANES:(s + 1) * LANES]) * (B_HD ** -0.5)).astype(BF16)

    lo = lax.broadcasted_iota(jnp.int32, (WINDOW + tb, LANES), 1) < half
    for src, dst in ((kx_ref, kvar_ref), (vx_ref, vvar_ref)):
        full = src[...]
        swapped = pltpu.roll(full, half, 1)
        dst[0] = jnp.where(lo, full, 0.0).astype(BF16)
        dst[1] = jnp.where(lo, 0.0, swapped).astype(BF16)
        dst[2] = jnp.where(lo, swapped, 0.0).astype(BF16)
        dst[3] = jnp.where(lo, 0.0, full).astype(BF16)

    first_rows = lax.broadcasted_iota(jnp.int32, (2 * c, 1), 0) < c
    kcol = lax.broadcasted_iota(jnp.int32, (2 * c, w), 1)

    sks = [jnp.where(first_rows, sinks_ref[4 * g + hh], sinks_ref[4 * g + 2 + hh])
           for g in range(B_KV_HEADS) for hh in range(2)]
    n_chunks = tb // c
    for i0 in range(0, n_chunks, SWA_LOCKSTEP):
        items = [(i, g, hh) for i in range(i0, min(i0 + SWA_LOCKSTEP, n_chunks))
                 for g in range(B_KV_HEADS) for hh in range(2)]
        lhs = {(i, g): jnp.concatenate([qs_ref[i * c:(i + 1) * c, (2 * g) * LANES:(2 * g + 1) * LANES],
                                        qs_ref[i * c:(i + 1) * c, (2 * g + 1) * LANES:(2 * g + 2) * LANES]], axis=0)
               for i, g, hh in items if hh == 0}
        s = [lax.dot_general(lhs[i, g], kvar_ref[2 * g + hh, i * c:i * c + w, :],
                             (((1,), (1,)), ((), ())), preferred_element_type=F32) for i, g, hh in items]
        if mask_start:
            s = [jnp.where(t * tb + i * c - WINDOW + kcol >= 0, si, -jnp.inf) for si, (i, g, hh) in zip(s, items)]
        m = [jnp.maximum(jnp.max(si, axis=-1, keepdims=True), sks[2 * g + hh]) for si, (i, g, hh) in zip(s, items)]
        p = [jnp.exp(si - mi) for si, mi in zip(s, m)]
        den = [jnp.sum(pi, axis=-1, keepdims=True) + jnp.exp(sks[2 * g + hh] - mi)
               for pi, mi, (i, g, hh) in zip(p, m, items)]
        pv = [jnp.dot(pi.astype(BF16), vvar_ref[2 * g + hh, i * c:i * c + w, :], preferred_element_type=F32) / di
              for pi, di, (i, g, hh) in zip(p, den, items)]
        for j in range(0, len(items), 2):
            i, g, _ = items[j]
            o = pv[j] + pv[j + 1]
            o_ref[i * c:(i + 1) * c, (2 * g) * LANES:(2 * g + 1) * LANES] = o[0:c, :]
            o_ref[i * c:(i + 1) * c, (2 * g + 1) * LANES:(2 * g + 2) * LANES] = o[c:2 * c, :]

    @pl.when(t == pl.num_programs(1) - 1)
    def _():
        newk_ref[...] = kx_ref[tb:tb + WINDOW, :]
        newv_ref[...] = vx_ref[tb:tb + WINDOW, :]

    if tb >= WINDOW:
        kx_ref[0:WINDOW, :] = kx_ref[tb:tb + WINDOW, :]
        vx_ref[0:WINDOW, :] = vx_ref[tb:tb + WINDOW, :]


def _swa(qb, kb, vb, cos, sa, sb, kc, vc, sinks, *, tb, c, mask_start):
    b, l, _ = qb.shape
    nt = l // tb
    assert tb >= WINDOW or nt == 1
    blk = lambda width: pl.BlockSpec((None, tb, width), lambda i, j: (i, j, 0))
    tab = pl.BlockSpec((tb, LANES), lambda i, j: (j, 0))
    cache = pl.BlockSpec((None, WINDOW, B_KW), lambda i, j: (i, 0, 0))
    return pl.pallas_call(
        functools.partial(_swa_body, tb=tb, c=c, mask_start=mask_start),
        grid=(b, nt),
        in_specs=[blk(B_QW), blk(B_KW), blk(B_KW), tab, tab, tab, cache, cache,
                  pl.BlockSpec(memory_space=pltpu.SMEM)],
        out_specs=[blk(B_QW), cache, cache],
        out_shape=[jax.ShapeDtypeStruct((b, l, B_QW), F32),
                   jax.ShapeDtypeStruct((b, WINDOW, B_KW), F32),
                   jax.ShapeDtypeStruct((b, WINDOW, B_KW), F32)],
        scratch_shapes=[pltpu.VMEM((WINDOW + tb, B_KW), F32),
                        pltpu.VMEM((WINDOW + tb, B_KW), F32),
                        pltpu.VMEM((tb, B_QW), BF16),
                        pltpu.VMEM((4, WINDOW + tb, LANES), BF16),
                        pltpu.VMEM((4, WINDOW + tb, LANES), BF16)],
        compiler_params=_params(("parallel", "arbitrary")),
        name="swa",
    )(qb, kb, vb, cos, sa, sb, kc, vc, sinks)


def _pack_w_in_tail(w):
    offs = np.cumsum([O_TAIL, A_HEADS, A_HEADS, B_QW, B_KW, B_KW]).tolist()
    bg, ag, qb, kb, vb = [w[..., offs[i]:offs[i + 1]] for i in range(5)]
    pad = lambda g: jnp.pad(g, ((0, 0), (0, 0), (0, GATE_W - g.shape[-1])))
    return jnp.concatenate([pad(bg), pad(ag), qb, kb, vb], axis=-1).astype(BF16)


def _pad_lanes(v):
    return jnp.pad(v, ((0, 0), (0, GATE_W - v.shape[-1])))[:, None, :]


def _trunk(x, rope_tabs, conv_bufs, s0s, k_caches, v_caches, wts, *, batch, length, tm, proj_tm,
           gdn_tb, gdn_c, swa_tb, swa_c, mask_start):
    (norm_ff1, ff1_in, ff1_out, norm_mix, w_in, w_in_tail, conv_w, alog, dtb, gnorm, sinks, w_out,
     norm_ff2, ff2_in, ff2_out, norm_final) = wts
    cos, sa, sb = rope_tabs
    lpad = -(-length // gdn_tb) * gdn_tb
    bufs, states, ks, vs = [], [], [], []

    def pad_rows(a):
        return a if lpad == length else jnp.pad(a, ((0, 0), (0, lpad - length), (0, 0)))

    for l in range(DEPTH):
        x = _ffn(x, norm_ff1, ff1_in, ff1_out, layer=l, tm=tm)
        act, z, bg, ag, qb, kb, vb, nbuf = _proj_in(x.reshape(batch, length, D_MODEL), norm_mix, w_in, w_in_tail,
                                                    conv_bufs[l], conv_w, layer=l, tm=proj_tm)
        o_a, s_new = _gdn(pad_rows(act), pad_rows(z), pad_rows(bg), pad_rows(ag), s0s[l], alog[l], dtb[l], gnorm[l],
                          tb=gdn_tb, c=gdn_c, n_valid=min(length, gdn_tb))
        o_b, nk, nv = _swa(qb, kb, vb, cos, sa, sb, k_caches[l], v_caches[l], sinks[l],
                           tb=swa_tb, c=swa_c, mask_start=mask_start)
        o_a = o_a[:, :length].reshape(batch * length, A_VW)
        x = _mix_out_ffn(x, o_a, o_b.reshape(batch * length, B_QW), w_out, norm_ff2, ff2_in, ff2_out,
                         norm_final if l == DEPTH - 1 else None, layer=l, tm=tm)
        bufs.append(nbuf)
        states.append(s_new)
        ks.append(nk.reshape(batch, WINDOW, B_KV_HEADS, B_HD))
        vs.append(nv.reshape(batch, WINDOW, B_KV_HEADS, B_HD))
    y = x.reshape(batch, length, D_MODEL)
    return y, jnp.stack(bufs), jnp.stack(states), jnp.stack(ks), jnp.stack(vs)


def kernel(x_prompt, x_sample, cache_conv, state_delta, cache_k, cache_v, norm_ff1, ff1_w_in, ff1_w_out, norm_mix, w_mix_in, conv_w, a_log, dt_bias, gnorm_w, sinks, w_mix_out, norm_ff2, ff2_w_in, ff2_w_out, norm_final):
    bp, lp, _ = x_prompt.shape
    bs, ls, _ = x_sample.shape
    rows = cache_k.shape[2]
    assert rows == WINDOW

    wts = (norm_ff1[:, None, :], ff1_w_in.astype(BF16), ff1_w_out.astype(BF16), norm_mix[:, None, :],
           w_mix_in.astype(BF16), _pack_w_in_tail(w_mix_in), conv_w, _pad_lanes(a_log), _pad_lanes(dt_bias), gnorm_w[:, None, :], sinks,
           w_mix_out.astype(BF16), norm_ff2[:, None, :], ff2_w_in.astype(BF16), ff2_w_out.astype(BF16),
           norm_final[None, :])

    inv = jnp.power(ROPE_THETA, -jnp.arange(0, ROPE_DIM, 2, dtype=F32) / ROPE_DIM)
    rot = ROPE_DIM // 2
    head_row = jnp.concatenate([inv, inv, jnp.zeros((B_HD - ROPE_DIM,), F32)])
    inv_row = jnp.tile(head_row, LANES // B_HD)[None, :]
    d = np.arange(LANES) % B_HD
    ma = jnp.asarray(((d >= rot) & (d < ROPE_DIM)).astype(np.float32))[None, :]
    mb = jnp.asarray((d < rot).astype(np.float32))[None, :]

    zero_buf = jnp.zeros((DEPTH, bp, CONV_W - 1, A_QKV), F32)
    zero_state = jnp.zeros((DEPTH, bp, A_HEADS, A_DK, A_DV), F32)
    zero_kv = jnp.zeros((DEPTH, bp, WINDOW, B_KW), F32)
    tabs_p = _rope_tables(inv_row, ma, mb, length=lp, pos0=0, tm=512)
    y_p, conv_p, delta_p, k_p, v_p = _trunk(
        x_prompt.reshape(bp * lp, D_MODEL), tabs_p, zero_buf, zero_state, zero_kv, zero_kv, wts,
        batch=bp, length=lp, tm=1024, proj_tm=512, gdn_tb=1024, gdn_c=128, swa_tb=512, swa_c=64, mask_start=True)

    tabs_s = _rope_tables(inv_row, ma, mb, length=ls, pos0=PAST_LEN, tm=ls)
    y_s, conv_s, delta_s, k_s, v_s = _trunk(
        x_sample.reshape(bs * ls, D_MODEL), tabs_s, cache_conv, state_delta,
        cache_k.reshape(DEPTH, bs, WINDOW, B_KW), cache_v.reshape(DEPTH, bs, WINDOW, B_KW), wts,
        batch=bs, length=ls, tm=bs * ls, proj_tm=ls, gdn_tb=128, gdn_c=128, swa_tb=ls, swa_c=ls, mask_start=False)
    return (y_p, y_s, conv_p, delta_p, k_p, v_p, conv_s, delta_s, k_s, v_s)
```

```python
import functools

import numpy as np
import jax
import jax.numpy as jnp
from jax import lax
from jax.experimental import pallas as pl
from jax.experimental.pallas import tpu as pltpu

F32 = jnp.float32
BF16 = jnp.bfloat16

D_MODEL = 1024
DEPTH = 4
PAST_LEN = 4096
EPS = 1e-6
A_HEADS = 4
A_DK = 128
A_DV = 128
A_QK = A_HEADS * A_DK
A_VW = A_HEADS * A_DV
A_QKV = 2 * A_QK + A_VW
CONV_W = 4
B_HEADS = 8
B_KV_HEADS = 2
B_HD = 64
B_QW = B_HEADS * B_HD
B_KW = B_KV_HEADS * B_HD
WINDOW = 128
ROPE_DIM = B_HD // 4
ROPE_THETA = 500000.0
D_MIX = A_VW + B_QW
D_FF = 2816

LANES = 128
SUBLANES = 8
GATE_W = LANES
VMEM_LIMIT = 56 * 1024 * 1024


def _params(sem, vmem=VMEM_LIMIT):
    return pltpu.CompilerParams(dimension_semantics=sem, vmem_limit_bytes=vmem)


def _resident(shape, layer=None):
    nd = len(shape)
    if layer is None:
        return pl.BlockSpec(shape, lambda *_: (0,) * nd, pipeline_mode=pl.Buffered(1))
    return pl.BlockSpec((None,) + tuple(shape), lambda *_: (layer,) + (0,) * nd, pipeline_mode=pl.Buffered(1))


def _rms(x, w):
    return x * lax.rsqrt(jnp.mean(x * x, axis=-1, keepdims=True) + EPS) * w


def _silu(x):
    return x * jax.nn.sigmoid(x)


def _mm(a, b):
    return jnp.dot(a.astype(BF16), b.astype(BF16), preferred_element_type=F32)


def _swiglu_residual(x, nw_ref, win_ref, wout_ref, fc):
    h = _rms(x, nw_ref[...]).astype(BF16)
    acc = None
    for c in range(0, D_FF, fc):
        gate = jnp.dot(h, win_ref[:, c:c + fc], preferred_element_type=F32)
        up = jnp.dot(h, win_ref[:, D_FF + c:D_FF + c + fc], preferred_element_type=F32)
        act = (_silu(gate) * up).astype(BF16)
        part = jnp.dot(act, wout_ref[c:c + fc, :], preferred_element_type=F32)
        acc = part if acc is None else acc + part
    return x + 0.5 * acc


FFN_ROWS = 512


def _row_groups(n):
    return [(r, min(FFN_ROWS, n - r)) for r in range(0, n, FFN_ROWS)]


def _ffn_body(x_ref, nw_ref, win_ref, wout_ref, o_ref, *, fc):
    for r, n in _row_groups(x_ref.shape[0]):
        o_ref[r:r + n, :] = _swiglu_residual(x_ref[r:r + n, :], nw_ref, win_ref, wout_ref, fc)


def _ffn(x, nw, w_in, w_out, *, layer, tm, fc=2816):
    m = x.shape[0]
    return pl.pallas_call(
        functools.partial(_ffn_body, fc=fc),
        grid=(m // tm,),
        in_specs=[pl.BlockSpec((tm, D_MODEL), lambda i: (i, 0)),
                  _resident((1, D_MODEL), layer),
                  _resident((D_MODEL, 2 * D_FF), layer),
                  _resident((D_FF, D_MODEL), layer)],
        out_specs=pl.BlockSpec((tm, D_MODEL), lambda i: (i, 0)),
        out_shape=jax.ShapeDtypeStruct((m, D_MODEL), F32),
        compiler_params=_params(("parallel",)),
        name="ffn",
    )(x, nw, w_in, w_out)


def _mix_out_ffn_body(x_ref, oa_ref, ob_ref, wo_ref, nw_ref, win_ref, wout_ref, *rest, fc, final):
    o_ref = rest[-1]
    for r, n in _row_groups(x_ref.shape[0]):
        x = (x_ref[r:r + n, :]
             + jnp.dot(oa_ref[r:r + n, :].astype(BF16), wo_ref[0:A_VW, :], preferred_element_type=F32)
             + jnp.dot(ob_ref[r:r + n, :].astype(BF16), wo_ref[A_VW:D_MIX, :], preferred_element_type=F32))
        y = _swiglu_residual(x, nw_ref, win_ref, wout_ref, fc)
        o_ref[r:r + n, :] = _rms(y, rest[0][...]) if final else y


def _mix_out_ffn(x, oa, ob, wo, nw, w_in, w_out, norm_final=None, *, layer, tm, fc=2816):
    m = x.shape[0]
    final = norm_final is not None
    row = lambda width: pl.BlockSpec((tm, width), lambda i: (i, 0))
    in_specs = [row(D_MODEL), row(A_VW), row(B_QW), _resident((D_MIX, D_MODEL), layer),
                _resident((1, D_MODEL), layer), _resident((D_MODEL, 2 * D_FF), layer),
                _resident((D_FF, D_MODEL), layer)]
    args = [x, oa, ob, wo, nw, w_in, w_out]
    if final:
        in_specs.append(_resident((1, D_MODEL)))
        args.append(norm_final)
    return pl.pallas_call(
        functools.partial(_mix_out_ffn_body, fc=fc, final=final),
        grid=(m // tm,),
        in_specs=in_specs,
        out_specs=row(D_MODEL),
        out_shape=jax.ShapeDtypeStruct((m, D_MODEL), F32),
        compiler_params=_params(("parallel",)),
        name="mix_out_ffn",
    )(*args)


O_TAIL = A_QKV + A_VW
_TAIL_GROUPS = (GATE_W, GATE_W, B_QW, B_KW, B_KW)
D_TAIL = sum(_TAIL_GROUPS)
HALO = CONV_W - 1
HALO_BASE = SUBLANES


def _proj_in_body(x_ref, nw_ref, w_ref, wt_ref, cbuf_ref, cw_ref, act_ref, z_ref, *rest, tm):
    tail_refs, (nbuf_ref, xp_ref) = rest[:-2], rest[-2:]
    j = pl.program_id(1)

    @pl.when(j == 0)
    def _():
        xp_ref[HALO_BASE - HALO:HALO_BASE, :] = cbuf_ref[...]

    h = _rms(x_ref[...], nw_ref[...]).astype(BF16)
    qkv = jnp.dot(h, w_ref[:, 0:A_QKV], preferred_element_type=F32)
    xp_ref[HALO_BASE:HALO_BASE + tm, :] = qkv
    z_ref[...] = jnp.dot(h, w_ref[:, A_QKV:O_TAIL], preferred_element_type=F32)
    off = 0
    for width, o_ref in zip(_TAIL_GROUPS, tail_refs):
        o_ref[...] = jnp.dot(h, wt_ref[:, off:off + width], preferred_element_type=F32)
        off += width

    cw = cw_ref[...]
    conv = qkv * cw[0:1, :]
    for t in range(1, CONV_W):
        conv = pltpu.roll(conv, 1, 0) + qkv * cw[t:t + 1, :]
    act_ref[...] = _silu(conv)
    lo = HALO_BASE - HALO
    head = xp_ref[lo:lo + SUBLANES, :] * cw[0:1, :]
    for t in range(1, CONV_W):
        head = head + xp_ref[lo + t:lo + t + SUBLANES, :] * cw[t:t + 1, :]
    act_ref[0:SUBLANES, :] = _silu(head)

    @pl.when(j == pl.num_programs(1) - 1)
    def _():
        nbuf_ref[...] = xp_ref[HALO_BASE + tm - HALO:HALO_BASE + tm, :]

    xp_ref[HALO_BASE - HALO:HALO_BASE, :] = xp_ref[HALO_BASE + tm - HALO:HALO_BASE + tm, :]


def _proj_in(x, nw, w, w_tail, cbuf, cw, *, layer, tm):
    b, l, _ = x.shape
    blk = lambda width: pl.BlockSpec((None, tm, width), lambda i, j: (i, j, 0))
    buf = pl.BlockSpec((None, HALO, A_QKV), lambda i, j: (i, 0, 0))
    widths = [A_QKV, A_VW] + list(_TAIL_GROUPS)
    return pl.pallas_call(
        functools.partial(_proj_in_body, tm=tm),
        grid=(b, l // tm),
        in_specs=[blk(D_MODEL), _resident((1, D_MODEL), layer), _resident((D_MODEL, O_TAIL), layer),
                  _resident((D_MODEL, D_TAIL), layer), buf, _resident((CONV_W, A_QKV), layer)],
        out_specs=[blk(width) for width in widths] + [buf],
        out_shape=[jax.ShapeDtypeStruct((b, l, width), F32) for width in widths]
                  + [jax.ShapeDtypeStruct((b, HALO, A_QKV), F32)],
        scratch_shapes=[pltpu.VMEM((HALO_BASE + tm, A_QKV), F32)],
        compiler_params=_params(("parallel", "arbitrary")),
        name="proj_in",
    )(x, nw, w, w_tail, cbuf, cw)


def _chunk_cumsum(g, tril):
    g1 = g.astype(BF16)
    r1 = g - g1.astype(F32)
    g2 = r1.astype(BF16)
    g3 = (r1 - g2.astype(F32)).astype(BF16)
    parts = jnp.dot(tril, jnp.concatenate([g1, g2, g3], axis=1), preferred_element_type=F32)
    return parts[:, 0:LANES] + parts[:, LANES:2 * LANES] + parts[:, 2 * LANES:3 * LANES]


INV_BLOCK = 32


def _unit_lower_inverses_minus_eye(ls, c):
    bs = min(INV_BLOCK, c)
    row = lax.broadcasted_iota(jnp.int32, (c, c), 0)
    col = lax.broadcasted_iota(jnp.int32, (c, c), 1)
    same = lambda b: (row // b) == (col // b)
    diag = [jnp.where(same(bs), l, 0.0) for l in ls] if bs < c else ls
    p = [-d for d in diag]
    m = [_mm(d, d) for d in diag]
    for _ in range(int(np.log2(bs)) - 2):
        pm = [_mm(pi, mi) for pi, mi in zip(p, m)]
        p = [pi + mi + pmi for pi, mi, pmi in zip(p, m, pm)]
        m = [_mm(mi, mi) for mi in m]
    pm = [_mm(pi, mi) for pi, mi in zip(p, m)]
    p = [pi + mi + pmi for pi, mi, pmi in zip(p, m, pm)]
    b = bs
    while b < c:
        sel = same(2 * b) & jnp.logical_not(same(b))
        off = [jnp.where(sel, l, 0.0) for l in ls]
        x = [oi + _mm(pi, oi) for pi, oi in zip(p, off)]
        p = [pi - (xi + _mm(xi, pi)) for pi, xi in zip(p, x)]
        b *= 2
    return p


def _gdn_body(act_ref, z_ref, bl_ref, al_ref, s0_ref, alog_ref, dtb_ref, gw_ref,
              o_ref, snew_ref, s_ref, u_ref, wq_ref, qk_ref, kdec_ref, *, tb, c, n_valid, n_streams):
    t = pl.program_id(1)
    seq = tb // n_streams

    @pl.when(t == 0)
    def _():
        s_ref[...] = s0_ref[...]

    beta_all = jax.nn.sigmoid(bl_ref[...])
    a_in = al_ref[...] + dtb_ref[...]
    softplus = jnp.maximum(a_in, 0.0) + jnp.log1p(jnp.exp(-jnp.abs(a_in)))
    g_all = -jnp.exp(alog_ref[...]) * softplus
    if n_valid < seq:
        live = lax.broadcasted_iota(jnp.int32, (tb, GATE_W), 0) % seq < n_valid
        beta_all = jnp.where(live, beta_all, 0.0)
        g_all = jnp.where(live, g_all, 0.0)

    row = lax.broadcasted_iota(jnp.int32, (c, c), 0)
    col = lax.broadcasted_iota(jnp.int32, (c, c), 1)
    causal = row >= col
    strict = row > col
    tril = jnp.where(causal, 1.0, 0.0).astype(BF16)
    gw = gw_ref[...]
    n_chunks = tb // c

    g_tots = []
    lmats, rhss = [], []
    for ci in range(n_chunks):
        r0 = ci * c
        beta = beta_all[r0:r0 + c, :]
        g_cum = _chunk_cumsum(g_all[r0:r0 + c, :], tril)
        g_cum_t = g_cum.T
        g_last = g_cum[c - 1:c, :]
        e_g = jnp.exp(g_cum)
        e_gl = jnp.exp(g_last - g_cum)
        g_tots.append(jnp.exp(g_last))
        for h in range(A_HEADS):
            i = ci * A_HEADS + h
            q = act_ref[r0:r0 + c, h * A_DK:(h + 1) * A_DK]
            k = act_ref[r0:r0 + c, A_QK + h * A_DK:A_QK + (h + 1) * A_DK]
            v = act_ref[r0:r0 + c, 2 * A_QK + h * A_DV:2 * A_QK + (h + 1) * A_DV]
            q = q * lax.rsqrt(jnp.sum(q * q, axis=-1, keepdims=True) + EPS) * (A_DK ** -0.5)
            k = k * lax.rsqrt(jnp.sum(k * k, axis=-1, keepdims=True) + EPS)
            b_col = beta[:, h:h + 1]
            eg_col = e_g[:, h:h + 1]
            decay = jnp.exp(jnp.where(causal, g_cum[:, h:h + 1] - g_cum_t[h:h + 1, :], -jnp.inf))
            k16 = k.astype(BF16)
            qk_kk = lax.dot_general(jnp.concatenate([q.astype(BF16), k16], axis=0), k16,
                                    (((1,), (1,)), ((), ())), preferred_element_type=F32)
            qk_ref[i] = (qk_kk[0:c, :] * decay).astype(BF16)
            lmats.append(jnp.where(strict, qk_kk[c:2 * c, :] * decay * b_col, 0.0))
            rhss.append(jnp.concatenate([v * b_col, k * (b_col * eg_col)], axis=1))
            wq_ref[i, c:2 * c, :] = (q * eg_col).astype(BF16)
            kdec_ref[i] = (k * e_gl[:, h:h + 1]).astype(BF16)
    tinvs = _unit_lower_inverses_minus_eye(lmats, c)
    for i, (tinv, rhs) in enumerate(zip(tinvs, rhss)):
        sol = rhs + _mm(tinv, rhs)
        u_ref[i] = sol[:, 0:A_DV]
        wq_ref[i, 0:c, :] = sol[:, A_DV:A_DV + A_DK].astype(BF16)

    for step in range(seq // c):
        lanes = [(sq, h, sq * (seq // c) + step) for sq in range(n_streams) for h in range(A_HEADS)]
        idx = [ci * A_HEADS + h for sq, h, ci in lanes]
        s_old = [s_ref[sq, h] for sq, h, ci in lanes]
        wq_s = [jnp.dot(wq_ref[i], st.astype(BF16), preferred_element_type=F32) for i, st in zip(idx, s_old)]
        v_new = [u_ref[i] - x[0:c, :] for i, x in zip(idx, wq_s)]
        v16 = [x.astype(BF16) for x in v_new]
        for k, (sq, h, ci) in enumerate(lanes):
            s_ref[sq, h] = s_old[k] * g_tots[ci][:, h:h + 1] + lax.dot_general(
                kdec_ref[idx[k]], v16[k], (((0,), (0,)), ((), ())), preferred_element_type=F32)
        for k, (sq, h, ci) in enumerate(lanes):
            r0 = ci * c
            o = wq_s[k][c:2 * c, :] + jnp.dot(qk_ref[idx[k]], v16[k], preferred_element_type=F32)
            o = o * lax.rsqrt(jnp.mean(o * o, axis=-1, keepdims=True) + EPS) * gw
            o_ref[r0:r0 + c, h * A_DV:(h + 1) * A_DV] = o * _silu(z_ref[r0:r0 + c, h * A_DV:(h + 1) * A_DV])

    @pl.when(t == pl.num_programs(1) - 1)
    def _():
        snew_ref[...] = s_ref[...]


def _gdn(act, z, bl, al, s0, alog, dtb, gw, *, tb, c, n_valid, n_streams=1):
    b, l, _ = act.shape
    nt = l // tb
    seq = tb // n_streams
    assert (n_valid == seq and n_streams == 1) or nt == 1
    n_items = (tb // c) * A_HEADS
    blk = lambda width: pl.BlockSpec((None, tb, width), lambda i, j: (i, j, 0))
    state = pl.BlockSpec((n_streams, A_HEADS, A_DK, A_DV), lambda i, j: (i, 0, 0, 0))
    lane_row = pl.BlockSpec((1, LANES), lambda i, j: (0, 0))
    return pl.pallas_call(
        functools.partial(_gdn_body, tb=tb, c=c, n_valid=n_valid, n_streams=n_streams),
        grid=(b, nt),
        in_specs=[blk(A_QKV), blk(A_VW), blk(GATE_W), blk(GATE_W), state, lane_row, lane_row, lane_row],
        out_specs=[blk(A_VW), state],
        out_shape=[jax.ShapeDtypeStruct((b, l, A_VW), F32),
                   jax.ShapeDtypeStruct((b * n_streams, A_HEADS, A_DK, A_DV), F32)],
        scratch_shapes=[pltpu.VMEM((n_streams, A_HEADS, A_DK, A_DV), F32),
                        pltpu.VMEM((n_items, c, A_DV), F32),
                        pltpu.VMEM((n_items, 2 * c, A_DK), BF16),
                        pltpu.VMEM((n_items, c, c), BF16),
                        pltpu.VMEM((n_items, c, A_DK), BF16)],
        compiler_params=_params(("parallel", "arbitrary")),
        name="gdn",
    )(act, z, bl, al, s0, alog, dtb, gw)


def _rope_table_body(inv_ref, ma_ref, mb_ref, cos_ref, sa_ref, sb_ref, c0_ref, s0_ref, *, tm, pos0):
    i = pl.program_id(0)
    inv = inv_ref[...]

    @pl.when(i == 0)
    def _():
        ang = lax.broadcasted_iota(jnp.int32, (tm, LANES), 0).astype(F32) * inv
        c0_ref[...] = jnp.cos(ang)
        s0_ref[...] = jnp.sin(ang)

    base = (pos0 + i * tm).astype(F32) * inv
    cb = jnp.cos(base)
    sn = jnp.sin(base)
    c0 = c0_ref[...]
    s0 = s0_ref[...]
    sin = s0 * cb + c0 * sn
    cos_ref[...] = c0 * cb - s0 * sn
    sa_ref[...] = sin * ma_ref[...]
    sb_ref[...] = -sin * mb_ref[...]


def _rope_tables(inv_row, ma, mb, *, length, pos0, tm):
    row = pl.BlockSpec((1, LANES), lambda i: (0, 0))
    out = pl.BlockSpec((tm, LANES), lambda i: (i, 0))
    return pl.pallas_call(
        functools.partial(_rope_table_body, tm=tm, pos0=pos0),
        grid=(length // tm,),
        in_specs=[row, row, row],
        out_specs=[out, out, out],
        out_shape=[jax.ShapeDtypeStruct((length, LANES), F32)] * 3,
        scratch_shapes=[pltpu.VMEM((tm, LANES), F32), pltpu.VMEM((tm, LANES), F32)],
        compiler_params=_params(("arbitrary",)),
        name="rope_tables",
    )(inv_row, ma, mb)


SWA_LOCKSTEP = 4


def _swa_body(qb_ref, kb_ref, vb_ref, cos_ref, sa_ref, sb_ref, kc_ref, vc_ref, sinks_ref,
              o_ref, newk_ref, newv_ref, kx_ref, vx_ref, qs_ref, kvar_ref, vvar_ref,
              *, tb, c, mask_start):
    t = pl.program_id(1)
    half = B_HD
    rot = ROPE_DIM // 2
    w = WINDOW + c

    @pl.when(t == 0)
    def _():
        kx_ref[0:WINDOW, :] = kc_ref[...]
        vx_ref[0:WINDOW, :] = vc_ref[...]

    cos = cos_ref[...]
    sa = sa_ref[...]
    sb = sb_ref[...]

    def rope(x):
        return x * cos + pltpu.roll(x, rot, 1) * sa + pltpu.roll(x, LANES - rot, 1) * sb

    kx_ref[WINDOW:WINDOW + tb, :] = rope(kb_ref[...])
    vx_ref[WINDOW:WINDOW + tb, :] = vb_ref[...]
    for s in range(B_QW // LANES):
        qs_ref[:, s * LANES:(s + 1) * LANES] = (
            rope(qb_ref[:, s * LANES:(s + 1) * LANES]) * (B_HD ** -0.5)).astype(BF16)

    lo = lax.broadcasted_iota(jnp.int32, (WINDOW + tb, LANES), 1) < half
    for src, dst in ((kx_ref, kvar_ref), (vx_ref, vvar_ref)):
        full = src[...]
        swapped = pltpu.roll(full, half, 1)
        dst[0] = jnp.where(lo, full, 0.0).astype(BF16)
        dst[1] = jnp.where(lo, 0.0, swapped).astype(BF16)
        dst[2] = jnp.where(lo, swapped, 0.0).astype(BF16)
        dst[3] = jnp.where(lo, 0.0, full).astype(BF16)

    first_rows = lax.broadcasted_iota(jnp.int32, (2 * c, 1), 0) < c
    kcol = lax.broadcasted_iota(jnp.int32, (2 * c, w), 1)

    sks = [jnp.where(first_rows, sinks_ref[4 * g + hh], sinks_ref[4 * g + 2 + hh])
           for g in range(B_KV_HEADS) for hh in range(2)]
    n_chunks = tb // c
    for i0 in range(0, n_chunks, SWA_LOCKSTEP):
        items = [(i, g, hh) for i in range(i0, min(i0 + SWA_LOCKSTEP, n_chunks))
                 for g in range(B_KV_HEADS) for hh in range(2)]
        lhs = {(i, g): jnp.concatenate([qs_ref[i * c:(i + 1) * c, (2 * g) * LANES:(2 * g + 1) * LANES],
                                        qs_ref[i * c:(i + 1) * c, (2 * g + 1) * LANES:(2 * g + 2) * LANES]], axis=0)
               for i, g, hh in items if hh == 0}
        s = [lax.dot_general(lhs[i, g], kvar_ref[2 * g + hh, i * c:i * c + w, :],
                             (((1,), (1,)), ((), ())), preferred_element_type=F32) for i, g, hh in items]
        if mask_start:
            s = [jnp.where(t * tb + i * c - WINDOW + kcol >= 0, si, -jnp.inf) for si, (i, g, hh) in zip(s, items)]
        m = [jnp.maximum(jnp.max(si, axis=-1, keepdims=True), sks[2 * g + hh]) for si, (i, g, hh) in zip(s, items)]
        p = [jnp.exp(si - mi) for si, mi in zip(s, m)]
        den = [jnp.sum(pi, axis=-1, keepdims=True) + jnp.exp(sks[2 * g + hh] - mi)
               for pi, mi, (i, g, hh) in zip(p, m, items)]
        pv = [jnp.dot(pi.astype(BF16), vvar_ref[2 * g + hh, i * c:i * c + w, :], preferred_element_type=F32) / di
              for pi, di, (i, g, hh) in zip(p, den, items)]
        for j in range(0, len(items), 2):
            i, g, _ = items[j]
            o = pv[j] + pv[j + 1]
            o_ref[i * c:(i + 1) * c, (2 * g) * LANES:(2 * g + 1) * LANES] = o[0:c, :]
            o_ref[i * c:(i + 1) * c, (2 * g + 1) * LANES:(2 * g + 2) * LANES] = o[c:2 * c, :]

    @pl.when(t == pl.num_programs(1) - 1)
    def _():
        newk_ref[...] = kx_ref[tb:tb + WINDOW, :]
        newv_ref[...] = vx_ref[tb:tb + WINDOW, :]

    if tb >= WINDOW:
        kx_ref[0:WINDOW, :] = kx_ref[tb:tb + WINDOW, :]
        vx_ref[0:WINDOW, :] = vx_ref[tb:tb + WINDOW, :]


def _swa(qb, kb, vb, cos, sa, sb, kc, vc, sinks, *, tb, c, mask_start):
    b, l, _ = qb.shape
    nt = l // tb
    assert tb >= WINDOW or nt == 1
    blk = lambda width: pl.BlockSpec((None, tb, width), lambda i, j: (i, j, 0))
    tab = pl.BlockSpec((tb, LANES), lambda i, j: (j, 0))
    cache = pl.BlockSpec((None, WINDOW, B_KW), lambda i, j: (i, 0, 0))
    return pl.pallas_call(
        functools.partial(_swa_body, tb=tb, c=c, mask_start=mask_start),
        grid=(b, nt),
        in_specs=[blk(B_QW), blk(B_KW), blk(B_KW), tab, tab, tab, cache, cache,
                  pl.BlockSpec(memory_space=pltpu.SMEM)],
        out_specs=[blk(B_QW), cache, cache],
        out_shape=[jax.ShapeDtypeStruct((b, l, B_QW), F32),
                   jax.ShapeDtypeStruct((b, WINDOW, B_KW), F32),
                   jax.ShapeDtypeStruct((b, WINDOW, B_KW), F32)],
        scratch_shapes=[pltpu.VMEM((WINDOW + tb, B_KW), F32),
                        pltpu.VMEM((WINDOW + tb, B_KW), F32),
                        pltpu.VMEM((tb, B_QW), BF16),
                        pltpu.VMEM((4, WINDOW + tb, LANES), BF16),
                        pltpu.VMEM((4, WINDOW + tb, LANES), BF16)],
        compiler_params=_params(("parallel", "arbitrary")),
        name="swa",
    )(qb, kb, vb, cos, sa, sb, kc, vc, sinks)


def _pack_w_in_tail(w):
    offs = np.cumsum([O_TAIL, A_HEADS, A_HEADS, B_QW, B_KW, B_KW]).tolist()
    bg, ag, qb, kb, vb = [w[..., offs[i]:offs[i + 1]] for i in range(5)]
    pad = lambda g: jnp.pad(g, ((0, 0), (0, 0), (0, GATE_W - g.shape[-1])))
    return jnp.concatenate([pad(bg), pad(ag), qb, kb, vb], axis=-1).astype(BF16)


def _pad_lanes(v):
    return jnp.pad(v, ((0, 0), (0, GATE_W - v.shape[-1])))[:, None, :]


def _trunk(x, rope_tabs, conv_bufs, s0s, k_caches, v_caches, wts, *, batch, length, tm, proj_tm,
           gdn_tb, gdn_c, swa_tb, swa_c, mask_start):
    (norm_ff1, ff1_in, ff1_out, norm_mix, w_in, w_in_tail, conv_w, alog, dtb, gnorm, sinks, w_out,
     norm_ff2, ff2_in, ff2_out, norm_final) = wts
    cos, sa, sb = rope_tabs
    lpad = -(-length // gdn_c) * gdn_c
    bufs, states, ks, vs = [], [], [], []

    def pad_rows(a):
        return a if lpad == length else jnp.pad(a, ((0, 0), (0, lpad - length), (0, 0)))

    for l in range(DEPTH):
        x = _ffn(x, norm_ff1, ff1_in, ff1_out, layer=l, tm=tm)
        act, z, bg, ag, qb, kb, vb, nbuf = _proj_in(x.reshape(batch, length, D_MODEL), norm_mix, w_in, w_in_tail,
                                                    conv_bufs[l], conv_w, layer=l, tm=proj_tm)
        if length < gdn_c:
            one = lambda a: pad_rows(a).reshape(1, batch * lpad, a.shape[-1])
            o_a, s_new = _gdn(one(act), one(z), one(bg), one(ag), s0s[l], alog[l], dtb[l], gnorm[l],
                              tb=batch * lpad, c=gdn_c, n_valid=length, n_streams=batch)
            o_a = o_a.reshape(batch, lpad, A_VW)
        else:
            o_a, s_new = _gdn(act, z, bg, ag, s0s[l], alog[l], dtb[l], gnorm[l], tb=gdn_tb, c=gdn_c, n_valid=gdn_tb)
        o_b, nk, nv = _swa(qb, kb, vb, cos, sa, sb, k_caches[l], v_caches[l], sinks[l],
                           tb=swa_tb, c=swa_c, mask_start=mask_start)
        o_a = o_a[:, :length].reshape(batch * length, A_VW)
        x = _mix_out_ffn(x, o_a, o_b.reshape(batch * length, B_QW), w_out, norm_ff2, ff2_in, ff2_out,
                         norm_final if l == DEPTH - 1 else None, layer=l, tm=tm)
        bufs.append(nbuf)
        states.append(s_new)
        ks.append(nk.reshape(batch, WINDOW, B_KV_HEADS, B_HD))
        vs.append(nv.reshape(batch, WINDOW, B_KV_HEADS, B_HD))
    y = x.reshape(batch, length, D_MODEL)
    return y, jnp.stack(bufs), jnp.stack(states), jnp.stack(ks), jnp.stack(vs)


def kernel(x_prompt, x_sample, cache_conv, state_delta, cache_k, cache_v, norm_ff1, ff1_w_in, ff1_w_out, norm_mix, w_mix_in, conv_w, a_log, dt_bias, gnorm_w, sinks, w_mix_out, norm_ff2, ff2_w_in, ff2_w_out, norm_final):
    bp, lp, _ = x_prompt.shape
    bs, ls, _ = x_sample.shape
    rows = cache_k.shape[2]
    assert rows == WINDOW

    wts = (norm_ff1[:, None, :], ff1_w_in.astype(BF16), ff1_w_out.astype(BF16), norm_mix[:, None, :],
           w_mix_in.astype(BF16), _pack_w_in_tail(w_mix_in), conv_w, _pad_lanes(a_log), _pad_lanes(dt_bias), gnorm_w[:, None, :], sinks,
           w_mix_out.astype(BF16), norm_ff2[:, None, :], ff2_w_in.astype(BF16), ff2_w_out.astype(BF16),
           norm_final[None, :])

    inv = jnp.power(ROPE_THETA, -jnp.arange(0, ROPE_DIM, 2, dtype=F32) / ROPE_DIM)
    rot = ROPE_DIM // 2
    head_row = jnp.concatenate([inv, inv, jnp.zeros((B_HD - ROPE_DIM,), F32)])
    inv_row = jnp.tile(head_row, LANES // B_HD)[None, :]
    d = np.arange(LANES) % B_HD
    ma = jnp.asarray(((d >= rot) & (d < ROPE_DIM)).astype(np.float32))[None, :]
    mb = jnp.asarray((d < rot).astype(np.float32))[None, :]

    zero_buf = jnp.zeros((DEPTH, bp, CONV_W - 1, A_QKV), F32)
    zero_state = jnp.zeros((DEPTH, bp, A_HEADS, A_DK, A_DV), F32)
    zero_kv = jnp.zeros((DEPTH, bp, WINDOW, B_KW), F32)
    tabs_p = _rope_tables(inv_row, ma, mb, length=lp, pos0=0, tm=512)
    y_p, conv_p, delta_p, k_p, v_p = _trunk(
        x_prompt.reshape(bp * lp, D_MODEL), tabs_p, zero_buf, zero_state, zero_kv, zero_kv, wts,
        batch=bp, length=lp, tm=1024, proj_tm=512, gdn_tb=1024, gdn_c=128, swa_tb=512, swa_c=64, mask_start=True)

    tabs_s = _rope_tables(inv_row, ma, mb, length=ls, pos0=PAST_LEN, tm=ls)
    y_s, conv_s, delta_s, k_s, v_s = _trunk(
        x_sample.reshape(bs * ls, D_MODEL), tabs_s, cache_conv, state_delta,
        cache_k.reshape(DEPTH, bs, WINDOW, B_KW), cache_v.reshape(DEPTH, bs, WINDOW, B_KW), wts,
        batch=bs, length=ls, tm=bs * ls, proj_tm=ls, gdn_tb=128, gdn_c=128, swa_tb=ls, swa_c=ls, mask_start=False)
    return (y_p, y_s, conv_p, delta_p, k_p, v_p, conv_s, delta_s, k_s, v_s)
```

```python
import functools

import numpy as np
import jax
import jax.numpy as jnp
from jax import lax
from jax.experimental import pallas as pl
from jax.experimental.pallas import tpu as pltpu

F32 = jnp.float32
BF16 = jnp.bfloat16

D_MODEL = 1024
DEPTH = 4
PAST_LEN = 4096
EPS = 1e-6
A_HEADS = 4
A_DK = 128
A_DV = 128
A_QK = A_HEADS * A_DK
A_VW = A_HEADS * A_DV
A_QKV = 2 * A_QK + A_VW
CONV_W = 4
B_HEADS = 8
B_KV_HEADS = 2
B_HD = 64
B_QW = B_HEADS * B_HD
B_KW = B_KV_HEADS * B_HD
WINDOW = 128
ROPE_DIM = B_HD // 4
ROPE_THETA = 500000.0
D_MIX = A_VW + B_QW
D_FF = 2816

LANES = 128
SUBLANES = 8
GATE_W = LANES
VMEM_LIMIT = 56 * 1024 * 1024


def _params(sem, vmem=VMEM_LIMIT):
    return pltpu.CompilerParams(dimension_semantics=sem, vmem_limit_bytes=vmem)


def _resident(shape, layer=None):
    nd = len(shape)
    if layer is None:
        return pl.BlockSpec(shape, lambda *_: (0,) * nd, pipeline_mode=pl.Buffered(1))
    return pl.BlockSpec((None,) + tuple(shape), lambda *_: (layer,) + (0,) * nd, pipeline_mode=pl.Buffered(1))


def _rms(x, w):
    return x * lax.rsqrt(jnp.mean(x * x, axis=-1, keepdims=True) + EPS) * w


def _silu(x):
    return x * jax.nn.sigmoid(x)


def _mm(a, b):
    return jnp.dot(a.astype(BF16), b.astype(BF16), preferred_element_type=F32)


def _swiglu_residual(x, nw_ref, win_ref, wout_ref, fc):
    h = _rms(x, nw_ref[...]).astype(BF16)
    acc = None
    for c in range(0, D_FF, fc):
        gate = jnp.dot(h, win_ref[:, c:c + fc], preferred_element_type=F32)
        up = jnp.dot(h, win_ref[:, D_FF + c:D_FF + c + fc], preferred_element_type=F32)
        act = (_silu(gate) * up).astype(BF16)
        part = jnp.dot(act, wout_ref[c:c + fc, :], preferred_element_type=F32)
        acc = part if acc is None else acc + part
    return x + 0.5 * acc


FFN_ROWS = 512


def _row_groups(n):
    return [(r, min(FFN_ROWS, n - r)) for r in range(0, n, FFN_ROWS)]


def _ffn_body(x_ref, nw_ref, win_ref, wout_ref, o_ref, *, fc):
    for r, n in _row_groups(x_ref.shape[0]):
        o_ref[r:r + n, :] = _swiglu_residual(x_ref[r:r + n, :], nw_ref, win_ref, wout_ref, fc)


def _ffn(x, nw, w_in, w_out, *, layer, tm, fc=2816):
    m = x.shape[0]
    return pl.pallas_call(
        functools.partial(_ffn_body, fc=fc),
        grid=(m // tm,),
        in_specs=[pl.BlockSpec((tm, D_MODEL), lambda i: (i, 0)),
                  _resident((1, D_MODEL), layer),
                  _resident((D_MODEL, 2 * D_FF), layer),
                  _resident((D_FF, D_MODEL), layer)],
        out_specs=pl.BlockSpec((tm, D_MODEL), lambda i: (i, 0)),
        out_shape=jax.ShapeDtypeStruct((m, D_MODEL), F32),
        compiler_params=_params(("parallel",)),
        name="ffn",
    )(x, nw, w_in, w_out)


def _mix_out_ffn_body(x_ref, oa_ref, ob_ref, wo_ref, nw_ref, win_ref, wout_ref, *rest, fc, final):
    o_ref = rest[-1]
    for r, n in _row_groups(x_ref.shape[0]):
        x = (x_ref[r:r + n, :]
             + jnp.dot(oa_ref[r:r + n, :].astype(BF16), wo_ref[0:A_VW, :], preferred_element_type=F32)
             + jnp.dot(ob_ref[r:r + n, :].astype(BF16), wo_ref[A_VW:D_MIX, :], preferred_element_type=F32))
        y = _swiglu_residual(x, nw_ref, win_ref, wout_ref, fc)
        o_ref[r:r + n, :] = _rms(y, rest[0][...]) if final else y


def _mix_out_ffn(x, oa, ob, wo, nw, w_in, w_out, norm_final=None, *, layer, tm, fc=2816):
    m = x.shape[0]
    final = norm_final is not None
    row = lambda width: pl.BlockSpec((tm, width), lambda i: (i, 0))
    in_specs = [row(D_MODEL), row(A_VW), row(B_QW), _resident((D_MIX, D_MODEL), layer),
                _resident((1, D_MODEL), layer), _resident((D_MODEL, 2 * D_FF), layer),
                _resident((D_FF, D_MODEL), layer)]
    args = [x, oa, ob, wo, nw, w_in, w_out]
    if final:
        in_specs.append(_resident((1, D_MODEL)))
        args.append(norm_final)
    return pl.pallas_call(
        functools.partial(_mix_out_ffn_body, fc=fc, final=final),
        grid=(m // tm,),
        in_specs=in_specs,
        out_specs=row(D_MODEL),
        out_shape=jax.ShapeDtypeStruct((m, D_MODEL), F32),
        compiler_params=_params(("parallel",)),
        name="mix_out_ffn",
    )(*args)


O_TAIL = A_QKV + A_VW
_TAIL_GROUPS = (GATE_W, GATE_W, B_QW, B_KW, B_KW)
D_TAIL = sum(_TAIL_GROUPS)
HALO = CONV_W - 1
HALO_BASE = SUBLANES


def _rope(x, cos, sa, sb):
    rot = ROPE_DIM // 2
    return x * cos + pltpu.roll(x, rot, 1) * sa + pltpu.roll(x, LANES - rot, 1) * sb


def _proj_in_body(x_ref, nw_ref, w_ref, wt_ref, cbuf_ref, cw_ref, cos_ref, sa_ref, sb_ref,
                  act_ref, z_ref, bg_ref, ag_ref, qb_ref, kb_ref, vb_ref, nbuf_ref, xp_ref, *, tm):
    j = pl.program_id(1)

    @pl.when(j == 0)
    def _():
        xp_ref[HALO_BASE - HALO:HALO_BASE, :] = cbuf_ref[...]

    h = _rms(x_ref[...], nw_ref[...]).astype(BF16)
    qkv = jnp.dot(h, w_ref[:, 0:A_QKV], preferred_element_type=F32)
    xp_ref[HALO_BASE:HALO_BASE + tm, :] = qkv
    z_ref[...] = jnp.dot(h, w_ref[:, A_QKV:O_TAIL], preferred_element_type=F32)
    offs = np.cumsum((0,) + _TAIL_GROUPS).tolist()
    bg, ag, qb, kb, vb = [jnp.dot(h, wt_ref[:, offs[i]:offs[i + 1]], preferred_element_type=F32)
                          for i in range(len(_TAIL_GROUPS))]
    bg_ref[...] = bg
    ag_ref[...] = ag
    vb_ref[...] = vb
    cos, sa, sb = cos_ref[...], sa_ref[...], sb_ref[...]
    kb_ref[...] = _rope(kb, cos, sa, sb)
    for s in range(B_QW // LANES):
        qb_ref[:, s * LANES:(s + 1) * LANES] = (
            _rope(qb[:, s * LANES:(s + 1) * LANES], cos, sa, sb) * (B_HD ** -0.5)).astype(BF16)

    cw = cw_ref[...]
    conv = qkv * cw[0:1, :]
    for t in range(1, CONV_W):
        conv = pltpu.roll(conv, 1, 0) + qkv * cw[t:t + 1, :]
    act_ref[...] = _silu(conv)
    lo = HALO_BASE - HALO
    head = xp_ref[lo:lo + SUBLANES, :] * cw[0:1, :]
    for t in range(1, CONV_W):
        head = head + xp_ref[lo + t:lo + t + SUBLANES, :] * cw[t:t + 1, :]
    act_ref[0:SUBLANES, :] = _silu(head)

    @pl.when(j == pl.num_programs(1) - 1)
    def _():
        nbuf_ref[...] = xp_ref[HALO_BASE + tm - HALO:HALO_BASE + tm, :]

    xp_ref[HALO_BASE - HALO:HALO_BASE, :] = xp_ref[HALO_BASE + tm - HALO:HALO_BASE + tm, :]


def _proj_in(x, nw, w, w_tail, cbuf, cw, cos, sa, sb, *, layer, tm):
    b, l, _ = x.shape
    blk = lambda width: pl.BlockSpec((None, tm, width), lambda i, j: (i, j, 0))
    buf = pl.BlockSpec((None, HALO, A_QKV), lambda i, j: (i, 0, 0))
    tab = pl.BlockSpec((tm, LANES), lambda i, j: (j, 0))
    widths = [A_QKV, A_VW] + list(_TAIL_GROUPS)
    dtypes = [F32, F32, F32, F32, BF16, F32, F32]
    return pl.pallas_call(
        functools.partial(_proj_in_body, tm=tm),
        grid=(b, l // tm),
        in_specs=[blk(D_MODEL), _resident((1, D_MODEL), layer), _resident((D_MODEL, O_TAIL), layer),
                  _resident((D_MODEL, D_TAIL), layer), buf, _resident((CONV_W, A_QKV), layer), tab, tab, tab],
        out_specs=[blk(width) for width in widths] + [buf],
        out_shape=[jax.ShapeDtypeStruct((b, l, width), dt) for width, dt in zip(widths, dtypes)]
                  + [jax.ShapeDtypeStruct((b, HALO, A_QKV), F32)],
        scratch_shapes=[pltpu.VMEM((HALO_BASE + tm, A_QKV), F32)],
        compiler_params=_params(("parallel", "arbitrary")),
        name="proj_in",
    )(x, nw, w, w_tail, cbuf, cw, cos, sa, sb)


def _chunk_cumsum(g, tril):
    g1 = g.astype(BF16)
    r1 = g - g1.astype(F32)
    g2 = r1.astype(BF16)
    g3 = (r1 - g2.astype(F32)).astype(BF16)
    parts = jnp.dot(tril, jnp.concatenate([g1, g2, g3], axis=1), preferred_element_type=F32)
    return parts[:, 0:LANES] + parts[:, LANES:2 * LANES] + parts[:, 2 * LANES:3 * LANES]


INV_BLOCK = 32


def _unit_lower_inverses_minus_eye(ls, c):
    bs = min(INV_BLOCK, c)
    row = lax.broadcasted_iota(jnp.int32, (c, c), 0)
    col = lax.broadcasted_iota(jnp.int32, (c, c), 1)
    same = lambda b: (row // b) == (col // b)
    diag = [jnp.where(same(bs), l, 0.0) for l in ls] if bs < c else ls
    p = [-d for d in diag]
    m = [_mm(d, d) for d in diag]
    for _ in range(int(np.log2(bs)) - 2):
        pm = [_mm(pi, mi) for pi, mi in zip(p, m)]
        p = [pi + mi + pmi for pi, mi, pmi in zip(p, m, pm)]
        m = [_mm(mi, mi) for mi in m]
    pm = [_mm(pi, mi) for pi, mi in zip(p, m)]
    p = [pi + mi + pmi for pi, mi, pmi in zip(p, m, pm)]
    b = bs
    while b < c:
        sel = same(2 * b) & jnp.logical_not(same(b))
        off = [jnp.where(sel, l, 0.0) for l in ls]
        x = [oi + _mm(pi, oi) for pi, oi in zip(p, off)]
        p = [pi - (xi + _mm(xi, pi)) for pi, xi in zip(p, x)]
        b *= 2
    return p


def _gdn_body(act_ref, z_ref, bl_ref, al_ref, s0_ref, alog_ref, dtb_ref, gw_ref,
              o_ref, snew_ref, s_ref, u_ref, wq_ref, qk_ref, kdec_ref, *, tb, c, n_valid, n_streams):
    t = pl.program_id(1)
    seq = tb // n_streams

    @pl.when(t == 0)
    def _():
        s_ref[...] = s0_ref[...]

    beta_all = jax.nn.sigmoid(bl_ref[...])
    a_in = al_ref[...] + dtb_ref[...]
    softplus = jnp.maximum(a_in, 0.0) + jnp.log1p(jnp.exp(-jnp.abs(a_in)))
    g_all = -jnp.exp(alog_ref[...]) * softplus
    if n_valid < seq:
        live = lax.broadcasted_iota(jnp.int32, (tb, GATE_W), 0) % seq < n_valid
        beta_all = jnp.where(live, beta_all, 0.0)
        g_all = jnp.where(live, g_all, 0.0)

    row = lax.broadcasted_iota(jnp.int32, (c, c), 0)
    col = lax.broadcasted_iota(jnp.int32, (c, c), 1)
    causal = row >= col
    strict = row > col
    tril = jnp.where(causal, 1.0, 0.0).astype(BF16)
    gw = gw_ref[...]
    n_chunks = tb // c

    g_tots = []
    lmats, rhss = [], []
    for ci in range(n_chunks):
        r0 = ci * c
        beta = beta_all[r0:r0 + c, :]
        g_cum = _chunk_cumsum(g_all[r0:r0 + c, :], tril)
        g_cum_t = g_cum.T
        g_last = g_cum[c - 1:c, :]
        e_g = jnp.exp(g_cum)
        e_gl = jnp.exp(g_last - g_cum)
        g_tots.append(jnp.exp(g_last))
        for h in range(A_HEADS):
            i = ci * A_HEADS + h
            q = act_ref[r0:r0 + c, h * A_DK:(h + 1) * A_DK]
            k = act_ref[r0:r0 + c, A_QK + h * A_DK:A_QK + (h + 1) * A_DK]
            v = act_ref[r0:r0 + c, 2 * A_QK + h * A_DV:2 * A_QK + (h + 1) * A_DV]
            q = q * lax.rsqrt(jnp.sum(q * q, axis=-1, keepdims=True) + EPS) * (A_DK ** -0.5)
            k = k * lax.rsqrt(jnp.sum(k * k, axis=-1, keepdims=True) + EPS)
            b_col = beta[:, h:h + 1]
            eg_col = e_g[:, h:h + 1]
            decay = jnp.exp(jnp.where(causal, g_cum[:, h:h + 1] - g_cum_t[h:h + 1, :], -jnp.inf))
            k16 = k.astype(BF16)
            qk_kk = lax.dot_general(jnp.concatenate([q.astype(BF16), k16], axis=0), k16,
                                    (((1,), (1,)), ((), ())), preferred_element_type=F32)
            qk_ref[i] = (qk_kk[0:c, :] * decay).astype(BF16)
            lmats.append(jnp.where(strict, qk_kk[c:2 * c, :] * decay * b_col, 0.0))
            rhss.append(jnp.concatenate([v * b_col, k * (b_col * eg_col)], axis=1))
            wq_ref[i, c:2 * c, :] = (q * eg_col).astype(BF16)
            kdec_ref[i] = (k * e_gl[:, h:h + 1]).astype(BF16)
    tinvs = _unit_lower_inverses_minus_eye(lmats, c)
    for i, (tinv, rhs) in enumerate(zip(tinvs, rhss)):
        sol = rhs + _mm(tinv, rhs)
        u_ref[i] = sol[:, 0:A_DV]
        wq_ref[i, 0:c, :] = sol[:, A_DV:A_DV + A_DK].astype(BF16)

    for step in range(seq // c):
        lanes = [(sq, h, sq * (seq // c) + step) for sq in range(n_streams) for h in range(A_HEADS)]
        idx = [ci * A_HEADS + h for sq, h, ci in lanes]
        s_old = [s_ref[sq, h] for sq, h, ci in lanes]
        wq_s = [jnp.dot(wq_ref[i], st.astype(BF16), preferred_element_type=F32) for i, st in zip(idx, s_old)]
        v_new = [u_ref[i] - x[0:c, :] for i, x in zip(idx, wq_s)]
        v16 = [x.astype(BF16) for x in v_new]
        for k, (sq, h, ci) in enumerate(lanes):
            s_ref[sq, h] = s_old[k] * g_tots[ci][:, h:h + 1] + lax.dot_general(
                kdec_ref[idx[k]], v16[k], (((0,), (0,)), ((), ())), preferred_element_type=F32)
        for k, (sq, h, ci) in enumerate(lanes):
            r0 = ci * c
            o = wq_s[k][c:2 * c, :] + jnp.dot(qk_ref[idx[k]], v16[k], preferred_element_type=F32)
            o = o * lax.rsqrt(jnp.mean(o * o, axis=-1, keepdims=True) + EPS) * gw
            o_ref[r0:r0 + c, h * A_DV:(h + 1) * A_DV] = o * _silu(z_ref[r0:r0 + c, h * A_DV:(h + 1) * A_DV])

    @pl.when(t == pl.num_programs(1) - 1)
    def _():
        snew_ref[...] = s_ref[...]


def _gdn(act, z, bl, al, s0, alog, dtb, gw, *, tb, c, n_valid, n_streams=1):
    b, l, _ = act.shape
    nt = l // tb
    seq = tb // n_streams
    assert (n_valid == seq and n_streams == 1) or nt == 1
    n_items = (tb // c) * A_HEADS
    blk = lambda width: pl.BlockSpec((None, tb, width), lambda i, j: (i, j, 0))
    state = pl.BlockSpec((n_streams, A_HEADS, A_DK, A_DV), lambda i, j: (i, 0, 0, 0))
    lane_row = pl.BlockSpec((1, LANES), lambda i, j: (0, 0))
    return pl.pallas_call(
        functools.partial(_gdn_body, tb=tb, c=c, n_valid=n_valid, n_streams=n_streams),
        grid=(b, nt),
        in_specs=[blk(A_QKV), blk(A_VW), blk(GATE_W), blk(GATE_W), state, lane_row, lane_row, lane_row],
        out_specs=[blk(A_VW), state],
        out_shape=[jax.ShapeDtypeStruct((b, l, A_VW), F32),
                   jax.ShapeDtypeStruct((b * n_streams, A_HEADS, A_DK, A_DV), F32)],
        scratch_shapes=[pltpu.VMEM((n_streams, A_HEADS, A_DK, A_DV), F32),
                        pltpu.VMEM((n_items, c, A_DV), F32),
                        pltpu.VMEM((n_items, 2 * c, A_DK), BF16),
                        pltpu.VMEM((n_items, c, c), BF16),
                        pltpu.VMEM((n_items, c, A_DK), BF16)],
        compiler_params=_params(("parallel", "arbitrary")),
        name="gdn",
    )(act, z, bl, al, s0, alog, dtb, gw)


def _rope_table_body(inv_ref, ma_ref, mb_ref, cos_ref, sa_ref, sb_ref, c0_ref, s0_ref, *, tm, pos0):
    i = pl.program_id(0)
    inv = inv_ref[...]

    @pl.when(i == 0)
    def _():
        ang = lax.broadcasted_iota(jnp.int32, (tm, LANES), 0).astype(F32) * inv
        c0_ref[...] = jnp.cos(ang)
        s0_ref[...] = jnp.sin(ang)

    base = (pos0 + i * tm).astype(F32) * inv
    cb = jnp.cos(base)
    sn = jnp.sin(base)
    c0 = c0_ref[...]
    s0 = s0_ref[...]
    sin = s0 * cb + c0 * sn
    cos_ref[...] = c0 * cb - s0 * sn
    sa_ref[...] = sin * ma_ref[...]
    sb_ref[...] = -sin * mb_ref[...]


def _rope_tables(inv_row, ma, mb, *, length, pos0, tm):
    row = pl.BlockSpec((1, LANES), lambda i: (0, 0))
    out = pl.BlockSpec((tm, LANES), lambda i: (i, 0))
    return pl.pallas_call(
        functools.partial(_rope_table_body, tm=tm, pos0=pos0),
        grid=(length // tm,),
        in_specs=[row, row, row],
        out_specs=[out, out, out],
        out_shape=[jax.ShapeDtypeStruct((length, LANES), F32)] * 3,
        scratch_shapes=[pltpu.VMEM((tm, LANES), F32), pltpu.VMEM((tm, LANES), F32)],
        compiler_params=_params(("arbitrary",)),
        name="rope_tables",
    )(inv_row, ma, mb)


SWA_LOCKSTEP = 4


def _swa_body(qs_ref, kb_ref, vb_ref, kc_ref, vc_ref, sinks_ref,
              o_ref, newk_ref, newv_ref, kx_ref, vx_ref, kvar_ref, vvar_ref, *, tb, c, mask_start):
    t = pl.program_id(1)
    half = B_HD
    w = WINDOW + c

    @pl.when(t == 0)
    def _():
        kx_ref[0:WINDOW, :] = kc_ref[...]
        vx_ref[0:WINDOW, :] = vc_ref[...]

    kx_ref[WINDOW:WINDOW + tb, :] = kb_ref[...]
    vx_ref[WINDOW:WINDOW + tb, :] = vb_ref[...]

    lo = lax.broadcasted_iota(jnp.int32, (WINDOW + tb, LANES), 1) < half
    for src, dst in ((kx_ref, kvar_ref), (vx_ref, vvar_ref)):
        full = src[...]
        swapped = pltpu.roll(full, half, 1)
        dst[0] = jnp.where(lo, full, 0.0).astype(BF16)
        dst[1] = jnp.where(lo, 0.0, swapped).astype(BF16)
        dst[2] = jnp.where(lo, swapped, 0.0).astype(BF16)
        dst[3] = jnp.where(lo, 0.0, full).astype(BF16)

    first_rows = lax.broadcasted_iota(jnp.int32, (2 * c, 1), 0) < c
    kcol = lax.broadcasted_iota(jnp.int32, (2 * c, w), 1)

    sks = [jnp.where(first_rows, sinks_ref[4 * g + hh], sinks_ref[4 * g + 2 + hh])
           for g in range(B_KV_HEADS) for hh in range(2)]
    n_chunks = tb // c
    for i0 in range(0, n_chunks, SWA_LOCKSTEP):
        items = [(i, g, hh) for i in range(i0, min(i0 + SWA_LOCKSTEP, n_chunks))
                 for g in range(B_KV_HEADS) for hh in range(2)]
        lhs = {(i, g): jnp.concatenate([qs_ref[i * c:(i + 1) * c, (2 * g) * LANES:(2 * g + 1) * LANES],
                                        qs_ref[i * c:(i + 1) * c, (2 * g + 1) * LANES:(2 * g + 2) * LANES]], axis=0)
               for i, g, hh in items if hh == 0}
        s = [lax.dot_general(lhs[i, g], kvar_ref[2 * g + hh, i * c:i * c + w, :],
                             (((1,), (1,)), ((), ())), preferred_element_type=F32) for i, g, hh in items]
        if mask_start:
            s = [jnp.where(t * tb + i * c - WINDOW + kcol >= 0, si, -jnp.inf) for si, (i, g, hh) in zip(s, items)]
        m = [jnp.maximum(jnp.max(si, axis=-1, keepdims=True), sks[2 * g + hh]) for si, (i, g, hh) in zip(s, items)]
        p = [jnp.exp(si - mi) for si, mi in zip(s, m)]
        den = [jnp.sum(pi, axis=-1, keepdims=True) + jnp.exp(sks[2 * g + hh] - mi)
               for pi, mi, (i, g, hh) in zip(p, m, items)]
        pv = [jnp.dot(pi.astype(BF16), vvar_ref[2 * g + hh, i * c:i * c + w, :], preferred_element_type=F32) / di
              for pi, di, (i, g, hh) in zip(p, den, items)]
        for j in range(0, len(items), 2):
            i, g, _ = items[j]
            o = pv[j] + pv[j + 1]
            o_ref[i * c:(i + 1) * c, (2 * g) * LANES:(2 * g + 1) * LANES] = o[0:c, :]
            o_ref[i * c:(i + 1) * c, (2 * g + 1) * LANES:(2 * g + 2) * LANES] = o[c:2 * c, :]

    @pl.when(t == pl.num_programs(1) - 1)
    def _():
        newk_ref[...] = kx_ref[tb:tb + WINDOW, :]
        newv_ref[...] = vx_ref[tb:tb + WINDOW, :]

    if tb >= WINDOW:
        kx_ref[0:WINDOW, :] = kx_ref[tb:tb + WINDOW, :]
        vx_ref[0:WINDOW, :] = vx_ref[tb:tb + WINDOW, :]


def _swa(qb, kb, vb, kc, vc, sinks, *, tb, c, mask_start):
    b, l, _ = qb.shape
    nt = l // tb
    assert tb >= WINDOW or nt == 1
    blk = lambda width: pl.BlockSpec((None, tb, width), lambda i, j: (i, j, 0))
    cache = pl.BlockSpec((None, WINDOW, B_KW), lambda i, j: (i, 0, 0))
    return pl.pallas_call(
        functools.partial(_swa_body, tb=tb, c=c, mask_start=mask_start),
        grid=(b, nt),
        in_specs=[blk(B_QW), blk(B_KW), blk(B_KW), cache, cache, pl.BlockSpec(memory_space=pltpu.SMEM)],
        out_specs=[blk(B_QW), cache, cache],
        out_shape=[jax.ShapeDtypeStruct((b, l, B_QW), F32),
                   jax.ShapeDtypeStruct((b, WINDOW, B_KW), F32),
                   jax.ShapeDtypeStruct((b, WINDOW, B_KW), F32)],
        scratch_shapes=[pltpu.VMEM((WINDOW + tb, B_KW), F32),
                        pltpu.VMEM((WINDOW + tb, B_KW), F32),
                        pltpu.VMEM((4, WINDOW + tb, LANES), BF16),
                        pltpu.VMEM((4, WINDOW + tb, LANES), BF16)],
        compiler_params=_params(("parallel", "arbitrary")),
        name="swa",
    )(qb, kb, vb, kc, vc, sinks)


def _pack_w_in_tail(w):
    offs = np.cumsum([O_TAIL, A_HEADS, A_HEADS, B_QW, B_KW, B_KW]).tolist()
    bg, ag, qb, kb, vb = [w[..., offs[i]:offs[i + 1]] for i in range(5)]
    pad = lambda g: jnp.pad(g, ((0, 0), (0, 0), (0, GATE_W - g.shape[-1])))
    return jnp.concatenate([pad(bg), pad(ag), qb, kb, vb], axis=-1).astype(BF16)


GDN_CHUNK = 128
SWA_CHUNK = 64


def _tiles(batch, length):
    if length >= 1024:
        return dict(tm=1024, proj_tm=512, gdn_tb=1024, gdn_c=GDN_CHUNK, swa_tb=512, swa_c=SWA_CHUNK)
    return dict(tm=batch * length, proj_tm=length, gdn_tb=GDN_CHUNK, gdn_c=GDN_CHUNK, swa_tb=length, swa_c=length)


def _pad_lanes(v):
    return jnp.pad(v, ((0, 0), (0, GATE_W - v.shape[-1])))[:, None, :]


def _trunk(x, rope_tabs, conv_bufs, s0s, k_caches, v_caches, wts, *, batch, length, tm, proj_tm,
           gdn_tb, gdn_c, swa_tb, swa_c, mask_start):
    (norm_ff1, ff1_in, ff1_out, norm_mix, w_in, w_in_tail, conv_w, alog, dtb, gnorm, sinks, w_out,
     norm_ff2, ff2_in, ff2_out, norm_final) = wts
    cos, sa, sb = rope_tabs
    lpad = -(-length // gdn_c) * gdn_c
    bufs, states, ks, vs = [], [], [], []

    def pad_rows(a):
        return a if lpad == length else jnp.pad(a, ((0, 0), (0, lpad - length), (0, 0)))

    for l in range(DEPTH):
        x = _ffn(x, norm_ff1, ff1_in, ff1_out, layer=l, tm=tm)
        act, z, bg, ag, qb, kb, vb, nbuf = _proj_in(x.reshape(batch, length, D_MODEL), norm_mix, w_in, w_in_tail,
                                                    conv_bufs[l], conv_w, cos, sa, sb, layer=l, tm=proj_tm)
        if length < gdn_c:
            one = lambda a: pad_rows(a).reshape(1, batch * lpad, a.shape[-1])
            o_a, s_new = _gdn(one(act), one(z), one(bg), one(ag), s0s[l], alog[l], dtb[l], gnorm[l],
                              tb=batch * lpad, c=gdn_c, n_valid=length, n_streams=batch)
            o_a = o_a.reshape(batch, lpad, A_VW)
        else:
            o_a, s_new = _gdn(act, z, bg, ag, s0s[l], alog[l], dtb[l], gnorm[l], tb=gdn_tb, c=gdn_c, n_valid=gdn_tb)
        o_b, nk, nv = _swa(qb, kb, vb, k_caches[l], v_caches[l], sinks[l],
                           tb=swa_tb, c=swa_c, mask_start=mask_start)
        o_a = o_a[:, :length].reshape(batch * length, A_VW)
        x = _mix_out_ffn(x, o_a, o_b.reshape(batch * length, B_QW), w_out, norm_ff2, ff2_in, ff2_out,
                         norm_final if l == DEPTH - 1 else None, layer=l, tm=tm)
        bufs.append(nbuf)
        states.append(s_new)
        ks.append(nk.reshape(batch, WINDOW, B_KV_HEADS, B_HD))
        vs.append(nv.reshape(batch, WINDOW, B_KV_HEADS, B_HD))
    y = x.reshape(batch, length, D_MODEL)
    return y, jnp.stack(bufs), jnp.stack(states), jnp.stack(ks), jnp.stack(vs)


def kernel(x_prompt, x_sample, cache_conv, state_delta, cache_k, cache_v, norm_ff1, ff1_w_in, ff1_w_out, norm_mix, w_mix_in, conv_w, a_log, dt_bias, gnorm_w, sinks, w_mix_out, norm_ff2, ff2_w_in, ff2_w_out, norm_final):
    bp, lp, _ = x_prompt.shape
    bs, ls, _ = x_sample.shape
    rows = cache_k.shape[2]
    assert rows == WINDOW

    wts = (norm_ff1[:, None, :], ff1_w_in.astype(BF16), ff1_w_out.astype(BF16), norm_mix[:, None, :],
           w_mix_in.astype(BF16), _pack_w_in_tail(w_mix_in), conv_w, _pad_lanes(a_log), _pad_lanes(dt_bias), gnorm_w[:, None, :], sinks,
           w_mix_out.astype(BF16), norm_ff2[:, None, :], ff2_w_in.astype(BF16), ff2_w_out.astype(BF16),
           norm_final[None, :])

    inv = jnp.power(ROPE_THETA, -jnp.arange(0, ROPE_DIM, 2, dtype=F32) / ROPE_DIM)
    rot = ROPE_DIM // 2
    head_row = jnp.concatenate([inv, inv, jnp.zeros((B_HD - ROPE_DIM,), F32)])
    inv_row = jnp.tile(head_row, LANES // B_HD)[None, :]
    d = np.arange(LANES) % B_HD
    ma = jnp.asarray(((d >= rot) & (d < ROPE_DIM)).astype(np.float32))[None, :]
    mb = jnp.asarray((d < rot).astype(np.float32))[None, :]

    zero_buf = jnp.zeros((DEPTH, bp, CONV_W - 1, A_QKV), F32)
    zero_state = jnp.zeros((DEPTH, bp, A_HEADS, A_DK, A_DV), F32)
    zero_kv = jnp.zeros((DEPTH, bp, WINDOW, B_KW), F32)
    tabs_p = _rope_tables(inv_row, ma, mb, length=lp, pos0=0, tm=512)
    y_p, conv_p, delta_p, k_p, v_p = _trunk(
        x_prompt.reshape(bp * lp, D_MODEL), tabs_p, zero_buf, zero_state, zero_kv, zero_kv, wts,
        batch=bp, length=lp, mask_start=True, **_tiles(bp, lp))

    tabs_s = _rope_tables(inv_row, ma, mb, length=ls, pos0=PAST_LEN, tm=ls)
    y_s, conv_s, delta_s, k_s, v_s = _trunk(
        x_sample.reshape(bs * ls, D_MODEL), tabs_s, cache_conv, state_delta,
        cache_k.reshape(DEPTH, bs, WINDOW, B_KW), cache_v.reshape(DEPTH, bs, WINDOW, B_KW), wts,
        batch=bs, length=ls, mask_start=False, **_tiles(bs, ls))
    return (y_p, y_s, conv_p, delta_p, k_p, v_p, conv_s, delta_s, k_s, v_s)
```

```python
import functools

import numpy as np
import jax
import jax.numpy as jnp
from jax import lax
from jax.experimental import pallas as pl
from jax.experimental.pallas import tpu as pltpu

F32 = jnp.float32
BF16 = jnp.bfloat16

D_MODEL = 1024
DEPTH = 4
PAST_LEN = 4096
EPS = 1e-6
A_HEADS = 4
A_DK = 128
A_DV = 128
A_QK = A_HEADS * A_DK
A_VW = A_HEADS * A_DV
A_QKV = 2 * A_QK + A_VW
CONV_W = 4
B_HEADS = 8
B_KV_HEADS = 2
B_HD = 64
B_QW = B_HEADS * B_HD
B_KW = B_KV_HEADS * B_HD
WINDOW = 128
ROPE_DIM = B_HD // 4
ROPE_THETA = 500000.0
D_MIX = A_VW + B_QW
D_FF = 2816

LANES = 128
SUBLANES = 8
GATE_W = LANES
VMEM_LIMIT = 56 * 1024 * 1024


def _params(sem, vmem=VMEM_LIMIT):
    return pltpu.CompilerParams(dimension_semantics=sem, vmem_limit_bytes=vmem)


def _resident(shape, layer=None):
    nd = len(shape)
    if layer is None:
        return pl.BlockSpec(shape, lambda *_: (0,) * nd, pipeline_mode=pl.Buffered(1))
    return pl.BlockSpec((None,) + tuple(shape), lambda *_: (layer,) + (0,) * nd, pipeline_mode=pl.Buffered(1))


def _rms(x, w):
    return x * lax.rsqrt(jnp.mean(x * x, axis=-1, keepdims=True) + EPS) * w


def _silu(x):
    return x * jax.nn.sigmoid(x)


def _mm(a, b):
    return jnp.dot(a.astype(BF16), b.astype(BF16), preferred_element_type=F32)


def _swiglu_residual(x, nw_ref, win_ref, wout_ref, fc):
    h = _rms(x, nw_ref[...]).astype(BF16)
    acc = None
    for c in range(0, D_FF, fc):
        gate = jnp.dot(h, win_ref[:, c:c + fc], preferred_element_type=F32)
        up = jnp.dot(h, win_ref[:, D_FF + c:D_FF + c + fc], preferred_element_type=F32)
        act = (_silu(gate) * up).astype(BF16)
        part = jnp.dot(act, wout_ref[c:c + fc, :], preferred_element_type=F32)
        acc = part if acc is None else acc + part
    return x + 0.5 * acc


FFN_ROWS = 512


def _row_groups(n):
    return [(r, min(FFN_ROWS, n - r)) for r in range(0, n, FFN_ROWS)]


def _ffn_body(x_ref, nw_ref, win_ref, wout_ref, o_ref, *, fc):
    for r, n in _row_groups(x_ref.shape[0]):
        o_ref[r:r + n, :] = _swiglu_residual(x_ref[r:r + n, :], nw_ref, win_ref, wout_ref, fc)


def _ffn(x, nw, w_in, w_out, *, layer, tm, fc=2816):
    m = x.shape[0]
    return pl.pallas_call(
        functools.partial(_ffn_body, fc=fc),
        grid=(m // tm,),
        in_specs=[pl.BlockSpec((tm, D_MODEL), lambda i: (i, 0)),
                  _resident((1, D_MODEL), layer),
                  _resident((D_MODEL, 2 * D_FF), layer),
                  _resident((D_FF, D_MODEL), layer)],
        out_specs=pl.BlockSpec((tm, D_MODEL), lambda i: (i, 0)),
        out_shape=jax.ShapeDtypeStruct((m, D_MODEL), F32),
        compiler_params=_params(("parallel",)),
        name="ffn",
    )(x, nw, w_in, w_out)


def _mix_out_ffn_body(x_ref, oa_ref, ob_ref, wo_ref, nw_ref, win_ref, wout_ref, *rest, fc, final):
    o_ref = rest[-1]
    for r, n in _row_groups(x_ref.shape[0]):
        x = (x_ref[r:r + n, :]
             + jnp.dot(oa_ref[r:r + n, :].astype(BF16), wo_ref[0:A_VW, :], preferred_element_type=F32)
             + jnp.dot(ob_ref[r:r + n, :].astype(BF16), wo_ref[A_VW:D_MIX, :], preferred_element_type=F32))
        y = _swiglu_residual(x, nw_ref, win_ref, wout_ref, fc)
        o_ref[r:r + n, :] = _rms(y, rest[0][...]) if final else y


def _mix_out_ffn(x, oa, ob, wo, nw, w_in, w_out, norm_final=None, *, layer, tm, fc=2816):
    m = x.shape[0]
    final = norm_final is not None
    row = lambda width: pl.BlockSpec((tm, width), lambda i: (i, 0))
    in_specs = [row(D_MODEL), row(A_VW), row(B_QW), _resident((D_MIX, D_MODEL), layer),
                _resident((1, D_MODEL), layer), _resident((D_MODEL, 2 * D_FF), layer),
                _resident((D_FF, D_MODEL), layer)]
    args = [x, oa, ob, wo, nw, w_in, w_out]
    if final:
        in_specs.append(_resident((1, D_MODEL)))
        args.append(norm_final)
    return pl.pallas_call(
        functools.partial(_mix_out_ffn_body, fc=fc, final=final),
        grid=(m // tm,),
        in_specs=in_specs,
        out_specs=row(D_MODEL),
        out_shape=jax.ShapeDtypeStruct((m, D_MODEL), F32),
        compiler_params=_params(("parallel",)),
        name="mix_out_ffn",
    )(*args)


O_TAIL = A_QKV + A_VW
_TAIL_GROUPS = (GATE_W, GATE_W, B_QW, B_KW, B_KW)
D_TAIL = sum(_TAIL_GROUPS)
HALO = CONV_W - 1
HALO_BASE = SUBLANES


def _rope(x, cos, sa, sb):
    rot = ROPE_DIM // 2
    return x * cos + pltpu.roll(x, rot, 1) * sa + pltpu.roll(x, LANES - rot, 1) * sb


def _proj_in_body(x_ref, nw_ref, w_ref, wt_ref, cbuf_ref, cw_ref, cos_ref, sa_ref, sb_ref, alog_ref, dtb_ref,
                  act_ref, z_ref, bg_ref, ag_ref, qb_ref, kb_ref, vb_ref, nbuf_ref, xp_ref, *, tm):
    j = pl.program_id(1)

    @pl.when(j == 0)
    def _():
        xp_ref[HALO_BASE - HALO:HALO_BASE, :] = cbuf_ref[...]

    h = _rms(x_ref[...], nw_ref[...]).astype(BF16)
    qkv = jnp.dot(h, w_ref[:, 0:A_QKV], preferred_element_type=F32)
    xp_ref[HALO_BASE:HALO_BASE + tm, :] = qkv
    z_ref[...] = jnp.dot(h, w_ref[:, A_QKV:O_TAIL], preferred_element_type=F32)
    offs = np.cumsum((0,) + _TAIL_GROUPS).tolist()
    bg, ag, qb, kb, vb = [jnp.dot(h, wt_ref[:, offs[i]:offs[i + 1]], preferred_element_type=F32)
                          for i in range(len(_TAIL_GROUPS))]
    bg_ref[...] = jax.nn.sigmoid(bg)
    a_in = ag + dtb_ref[...]
    softplus = jnp.maximum(a_in, 0.0) + jnp.log1p(jnp.exp(-jnp.abs(a_in)))
    ag_ref[...] = -jnp.exp(alog_ref[...]) * softplus
    vb_ref[...] = vb
    cos, sa, sb = cos_ref[...], sa_ref[...], sb_ref[...]
    kb_ref[...] = _rope(kb, cos, sa, sb)
    for s in range(B_QW // LANES):
        qb_ref[:, s * LANES:(s + 1) * LANES] = (
            _rope(qb[:, s * LANES:(s + 1) * LANES], cos, sa, sb) * (B_HD ** -0.5)).astype(BF16)

    cw = cw_ref[...]
    conv = qkv * cw[0:1, :]
    for t in range(1, CONV_W):
        conv = pltpu.roll(conv, 1, 0) + qkv * cw[t:t + 1, :]
    act_ref[...] = _silu(conv)
    lo = HALO_BASE - HALO
    head = xp_ref[lo:lo + SUBLANES, :] * cw[0:1, :]
    for t in range(1, CONV_W):
        head = head + xp_ref[lo + t:lo + t + SUBLANES, :] * cw[t:t + 1, :]
    act_ref[0:SUBLANES, :] = _silu(head)
    for hd in range(2 * A_HEADS):
        cols = slice(hd * A_DK, (hd + 1) * A_DK)
        t = act_ref[:, cols]
        t = t * lax.rsqrt(jnp.sum(t * t, axis=-1, keepdims=True) + EPS)
        act_ref[:, cols] = t * (A_DK ** -0.5) if hd < A_HEADS else t

    @pl.when(j == pl.num_programs(1) - 1)
    def _():
        nbuf_ref[...] = xp_ref[HALO_BASE + tm - HALO:HALO_BASE + tm, :]

    xp_ref[HALO_BASE - HALO:HALO_BASE, :] = xp_ref[HALO_BASE + tm - HALO:HALO_BASE + tm, :]


def _proj_in(x, nw, w, w_tail, cbuf, cw, cos, sa, sb, alog, dtb, *, layer, tm):
    b, l, _ = x.shape
    blk = lambda width: pl.BlockSpec((None, tm, width), lambda i, j: (i, j, 0))
    buf = pl.BlockSpec((None, HALO, A_QKV), lambda i, j: (i, 0, 0))
    tab = pl.BlockSpec((tm, LANES), lambda i, j: (j, 0))
    widths = [A_QKV, A_VW] + list(_TAIL_GROUPS)
    dtypes = [F32, F32, F32, F32, BF16, F32, F32]
    return pl.pallas_call(
        functools.partial(_proj_in_body, tm=tm),
        grid=(b, l // tm),
        in_specs=[blk(D_MODEL), _resident((1, D_MODEL), layer), _resident((D_MODEL, O_TAIL), layer),
                  _resident((D_MODEL, D_TAIL), layer), buf, _resident((CONV_W, A_QKV), layer), tab, tab, tab,
                  _resident((1, GATE_W), layer), _resident((1, GATE_W), layer)],
        out_specs=[blk(width) for width in widths] + [buf],
        out_shape=[jax.ShapeDtypeStruct((b, l, width), dt) for width, dt in zip(widths, dtypes)]
                  + [jax.ShapeDtypeStruct((b, HALO, A_QKV), F32)],
        scratch_shapes=[pltpu.VMEM((HALO_BASE + tm, A_QKV), F32)],
        compiler_params=_params(("parallel", "arbitrary")),
        name="proj_in",
    )(x, nw, w, w_tail, cbuf, cw, cos, sa, sb, alog, dtb)


def _chunk_cumsum(g, tril):
    g1 = g.astype(BF16)
    r1 = g - g1.astype(F32)
    g2 = r1.astype(BF16)
    g3 = (r1 - g2.astype(F32)).astype(BF16)
    parts = jnp.dot(tril, jnp.concatenate([g1, g2, g3], axis=1), preferred_element_type=F32)
    return parts[:, 0:LANES] + parts[:, LANES:2 * LANES] + parts[:, 2 * LANES:3 * LANES]


INV_BLOCK = 32


def _unit_lower_inverses_minus_eye(ls, c):
    bs = min(INV_BLOCK, c)
    row = lax.broadcasted_iota(jnp.int32, (c, c), 0)
    col = lax.broadcasted_iota(jnp.int32, (c, c), 1)
    same = lambda b: (row // b) == (col // b)
    diag = [jnp.where(same(bs), l, 0.0) for l in ls] if bs < c else ls
    p = [-d for d in diag]
    m = [_mm(d, d) for d in diag]
    for _ in range(int(np.log2(bs)) - 2):
        pm = [_mm(pi, mi) for pi, mi in zip(p, m)]
        p = [pi + mi + pmi for pi, mi, pmi in zip(p, m, pm)]
        m = [_mm(mi, mi) for mi in m]
    pm = [_mm(pi, mi) for pi, mi in zip(p, m)]
    p = [pi + mi + pmi for pi, mi, pmi in zip(p, m, pm)]
    b = bs
    while b < c:
        sel = same(2 * b) & jnp.logical_not(same(b))
        off = [jnp.where(sel, l, 0.0) for l in ls]
        x = [oi + _mm(pi, oi) for pi, oi in zip(p, off)]
        p = [pi - (xi + _mm(xi, pi)) for pi, xi in zip(p, x)]
        b *= 2
    return p


def _gdn_body(act_ref, z_ref, beta_ref, g_ref, s0_ref, gw_ref,
              o_ref, snew_ref, s_ref, u_ref, wq_ref, qk_ref, kdec_ref, *, tb, c, n_streams):
    t = pl.program_id(1)
    seq = tb // n_streams

    @pl.when(t == 0)
    def _():
        s_ref[...] = s0_ref[...]

    beta_all = beta_ref[...]
    g_all = g_ref[...]

    row = lax.broadcasted_iota(jnp.int32, (c, c), 0)
    col = lax.broadcasted_iota(jnp.int32, (c, c), 1)
    causal = row >= col
    strict = row > col
    tril = jnp.where(causal, 1.0, 0.0).astype(BF16)
    gw = gw_ref[...]
    n_chunks = tb // c

    g_tots = []
    lmats, rhss = [], []
    for ci in range(n_chunks):
        r0 = ci * c
        beta = beta_all[r0:r0 + c, :]
        g_cum = _chunk_cumsum(g_all[r0:r0 + c, :], tril)
        g_cum_t = g_cum.T
        g_last = g_cum[c - 1:c, :]
        e_g = jnp.exp(g_cum)
        e_gl = jnp.exp(g_last - g_cum)
        g_tots.append(jnp.exp(g_last))
        for h in range(A_HEADS):
            i = ci * A_HEADS + h
            q = act_ref[r0:r0 + c, h * A_DK:(h + 1) * A_DK]
            k = act_ref[r0:r0 + c, A_QK + h * A_DK:A_QK + (h + 1) * A_DK]
            v = act_ref[r0:r0 + c, 2 * A_QK + h * A_DV:2 * A_QK + (h + 1) * A_DV]
            b_col = beta[:, h:h + 1]
            eg_col = e_g[:, h:h + 1]
            decay = jnp.exp(jnp.where(causal, g_cum[:, h:h + 1] - g_cum_t[h:h + 1, :], -jnp.inf))
            k16 = k.astype(BF16)
            qk_kk = lax.dot_general(jnp.concatenate([q.astype(BF16), k16], axis=0), k16,
                                    (((1,), (1,)), ((), ())), preferred_element_type=F32)
            qk_ref[i] = (qk_kk[0:c, :] * decay).astype(BF16)
            lmats.append(jnp.where(strict, qk_kk[c:2 * c, :] * decay * b_col, 0.0))
            rhss.append(jnp.concatenate([v * b_col, k * (b_col * eg_col)], axis=1))
            wq_ref[i, c:2 * c, :] = (q * eg_col).astype(BF16)
            kdec_ref[i] = (k * e_gl[:, h:h + 1]).astype(BF16)
    tinvs = _unit_lower_inverses_minus_eye(lmats, c)
    for i, (tinv, rhs) in enumerate(zip(tinvs, rhss)):
        sol = rhs + _mm(tinv, rhs)
        u_ref[i] = sol[:, 0:A_DV]
        wq_ref[i, 0:c, :] = sol[:, A_DV:A_DV + A_DK].astype(BF16)

    for step in range(seq // c):
        lanes = [(sq, h, sq * (seq // c) + step) for sq in range(n_streams) for h in range(A_HEADS)]
        idx = [ci * A_HEADS + h for sq, h, ci in lanes]
        s_old = [s_ref[sq, h] for sq, h, ci in lanes]
        wq_s = [jnp.dot(wq_ref[i], st.astype(BF16), preferred_element_type=F32) for i, st in zip(idx, s_old)]
        v_new = [u_ref[i] - x[0:c, :] for i, x in zip(idx, wq_s)]
        v16 = [x.astype(BF16) for x in v_new]
        for k, (sq, h, ci) in enumerate(lanes):
            s_ref[sq, h] = s_old[k] * g_tots[ci][:, h:h + 1] + lax.dot_general(
                kdec_ref[idx[k]], v16[k], (((0,), (0,)), ((), ())), preferred_element_type=F32)
        for k, (sq, h, ci) in enumerate(lanes):
            r0 = ci * c
            o = wq_s[k][c:2 * c, :] + jnp.dot(qk_ref[idx[k]], v16[k], preferred_element_type=F32)
            o = o * lax.rsqrt(jnp.mean(o * o, axis=-1, keepdims=True) + EPS) * gw
            o_ref[r0:r0 + c, h * A_DV:(h + 1) * A_DV] = o * _silu(z_ref[r0:r0 + c, h * A_DV:(h + 1) * A_DV])

    @pl.when(t == pl.num_programs(1) - 1)
    def _():
        snew_ref[...] = s_ref[...]


def _gdn(act, z, beta, g, s0, gw, *, tb, c, n_streams=1):
    b, l, _ = act.shape
    nt = l // tb
    assert n_streams == 1 or nt == 1
    n_items = (tb // c) * A_HEADS
    blk = lambda width: pl.BlockSpec((None, tb, width), lambda i, j: (i, j, 0))
    state = pl.BlockSpec((n_streams, A_HEADS, A_DK, A_DV), lambda i, j: (i, 0, 0, 0))
    lane_row = pl.BlockSpec((1, LANES), lambda i, j: (0, 0))
    return pl.pallas_call(
        functools.partial(_gdn_body, tb=tb, c=c, n_streams=n_streams),
        grid=(b, nt),
        in_specs=[blk(A_QKV), blk(A_VW), blk(GATE_W), blk(GATE_W), state, lane_row],
        out_specs=[blk(A_VW), state],
        out_shape=[jax.ShapeDtypeStruct((b, l, A_VW), F32),
                   jax.ShapeDtypeStruct((b * n_streams, A_HEADS, A_DK, A_DV), F32)],
        scratch_shapes=[pltpu.VMEM((n_streams, A_HEADS, A_DK, A_DV), F32),
                        pltpu.VMEM((n_items, c, A_DV), F32),
                        pltpu.VMEM((n_items, 2 * c, A_DK), BF16),
                        pltpu.VMEM((n_items, c, c), BF16),
                        pltpu.VMEM((n_items, c, A_DK), BF16)],
        compiler_params=_params(("parallel", "arbitrary")),
        name="gdn",
    )(act, z, beta, g, s0, gw)


def _rope_table_body(inv_ref, ma_ref, mb_ref, cos_ref, sa_ref, sb_ref, c0_ref, s0_ref, *, tm, pos0):
    i = pl.program_id(0)
    inv = inv_ref[...]

    @pl.when(i == 0)
    def _():
        ang = lax.broadcasted_iota(jnp.int32, (tm, LANES), 0).astype(F32) * inv
        c0_ref[...] = jnp.cos(ang)
        s0_ref[...] = jnp.sin(ang)

    base = (pos0 + i * tm).astype(F32) * inv
    cb = jnp.cos(base)
    sn = jnp.sin(base)
    c0 = c0_ref[...]
    s0 = s0_ref[...]
    sin = s0 * cb + c0 * sn
    cos_ref[...] = c0 * cb - s0 * sn
    sa_ref[...] = sin * ma_ref[...]
    sb_ref[...] = -sin * mb_ref[...]


def _rope_tables(inv_row, ma, mb, *, length, pos0, tm):
    row = pl.BlockSpec((1, LANES), lambda i: (0, 0))
    out = pl.BlockSpec((tm, LANES), lambda i: (i, 0))
    return pl.pallas_call(
        functools.partial(_rope_table_body, tm=tm, pos0=pos0),
        grid=(length // tm,),
        in_specs=[row, row, row],
        out_specs=[out, out, out],
        out_shape=[jax.ShapeDtypeStruct((length, LANES), F32)] * 3,
        scratch_shapes=[pltpu.VMEM((tm, LANES), F32), pltpu.VMEM((tm, LANES), F32)],
        compiler_params=_params(("arbitrary",)),
        name="rope_tables",
    )(inv_row, ma, mb)


SWA_LOCKSTEP = 4


def _swa_body(qs_ref, kb_ref, vb_ref, kc_ref, vc_ref, sinks_ref,
              o_ref, newk_ref, newv_ref, kx_ref, vx_ref, kvar_ref, vvar_ref, *, tb, c, mask_start):
    t = pl.program_id(1)
    half = B_HD
    w = WINDOW + c

    @pl.when(t == 0)
    def _():
        kx_ref[0:WINDOW, :] = kc_ref[...]
        vx_ref[0:WINDOW, :] = vc_ref[...]

    kx_ref[WINDOW:WINDOW + tb, :] = kb_ref[...]
    vx_ref[WINDOW:WINDOW + tb, :] = vb_ref[...]

    lo = lax.broadcasted_iota(jnp.int32, (WINDOW + tb, LANES), 1) < half
    for src, dst in ((kx_ref, kvar_ref), (vx_ref, vvar_ref)):
        full = src[...]
        swapped = pltpu.roll(full, half, 1)
        dst[0] = jnp.where(lo, full, 0.0).astype(BF16)
        dst[1] = jnp.where(lo, 0.0, swapped).astype(BF16)
        dst[2] = jnp.where(lo, swapped, 0.0).astype(BF16)
        dst[3] = jnp.where(lo, 0.0, full).astype(BF16)

    first_rows = lax.broadcasted_iota(jnp.int32, (2 * c, 1), 0) < c
    kcol = lax.broadcasted_iota(jnp.int32, (2 * c, w), 1)

    sks = [jnp.where(first_rows, sinks_ref[4 * g + hh], sinks_ref[4 * g + 2 + hh])
           for g in range(B_KV_HEADS) for hh in range(2)]
    n_chunks = tb // c
    for i0 in range(0, n_chunks, SWA_LOCKSTEP):
        items = [(i, g, hh) for i in range(i0, min(i0 + SWA_LOCKSTEP, n_chunks))
                 for g in range(B_KV_HEADS) for hh in range(2)]
        lhs = {(i, g): jnp.concatenate([qs_ref[i * c:(i + 1) * c, (2 * g) * LANES:(2 * g + 1) * LANES],
                                        qs_ref[i * c:(i + 1) * c, (2 * g + 1) * LANES:(2 * g + 2) * LANES]], axis=0)
               for i, g, hh in items if hh == 0}
        s = [lax.dot_general(lhs[i, g], kvar_ref[2 * g + hh, i * c:i * c + w, :],
                             (((1,), (1,)), ((), ())), preferred_element_type=F32) for i, g, hh in items]
        if mask_start:
            s = [jnp.where(t * tb + i * c - WINDOW + kcol >= 0, si, -jnp.inf) for si, (i, g, hh) in zip(s, items)]
        m = [jnp.maximum(jnp.max(si, axis=-1, keepdims=True), sks[2 * g + hh]) for si, (i, g, hh) in zip(s, items)]
        p = [jnp.exp(si - mi) for si, mi in zip(s, m)]
        den = [jnp.sum(pi, axis=-1, keepdims=True) + jnp.exp(sks[2 * g + hh] - mi)
               for pi, mi, (i, g, hh) in zip(p, m, items)]
        pv = [jnp.dot(pi.astype(BF16), vvar_ref[2 * g + hh, i * c:i * c + w, :], preferred_element_type=F32) / di
              for pi, di, (i, g, hh) in zip(p, den, items)]
        for j in range(0, len(items), 2):
            i, g, _ = items[j]
            o = pv[j] + pv[j + 1]
            o_ref[i * c:(i + 1) * c, (2 * g) * LANES:(2 * g + 1) * LANES] = o[0:c, :]
            o_ref[i * c:(i + 1) * c, (2 * g + 1) * LANES:(2 * g + 2) * LANES] = o[c:2 * c, :]

    @pl.when(t == pl.num_programs(1) - 1)
    def _():
        newk_ref[...] = kx_ref[tb:tb + WINDOW, :]
        newv_ref[...] = vx_ref[tb:tb + WINDOW, :]

    if tb >= WINDOW:
        kx_ref[0:WINDOW, :] = kx_ref[tb:tb + WINDOW, :]
        vx_ref[0:WINDOW, :] = vx_ref[tb:tb + WINDOW, :]


def _swa(qb, kb, vb, kc, vc, sinks, *, tb, c, mask_start):
    b, l, _ = qb.shape
    nt = l // tb
    assert tb >= WINDOW or nt == 1
    blk = lambda width: pl.BlockSpec((None, tb, width), lambda i, j: (i, j, 0))
    cache = pl.BlockSpec((None, WINDOW, B_KW), lambda i, j: (i, 0, 0))
    return pl.pallas_call(
        functools.partial(_swa_body, tb=tb, c=c, mask_start=mask_start),
        grid=(b, nt),
        in_specs=[blk(B_QW), blk(B_KW), blk(B_KW), cache, cache, pl.BlockSpec(memory_space=pltpu.SMEM)],
        out_specs=[blk(B_QW), cache, cache],
        out_shape=[jax.ShapeDtypeStruct((b, l, B_QW), F32),
                   jax.ShapeDtypeStruct((b, WINDOW, B_KW), F32),
                   jax.ShapeDtypeStruct((b, WINDOW, B_KW), F32)],
        scratch_shapes=[pltpu.VMEM((WINDOW + tb, B_KW), F32),
                        pltpu.VMEM((WINDOW + tb, B_KW), F32),
                        pltpu.VMEM((4, WINDOW + tb, LANES), BF16),
                        pltpu.VMEM((4, WINDOW + tb, LANES), BF16)],
        compiler_params=_params(("parallel", "arbitrary")),
        name="swa",
    )(qb, kb, vb, kc, vc, sinks)


def _pack_w_in_tail(w):
    offs = np.cumsum([O_TAIL, A_HEADS, A_HEADS, B_QW, B_KW, B_KW]).tolist()
    bg, ag, qb, kb, vb = [w[..., offs[i]:offs[i + 1]] for i in range(5)]
    pad = lambda g: jnp.pad(g, ((0, 0), (0, 0), (0, GATE_W - g.shape[-1])))
    return jnp.concatenate([pad(bg), pad(ag), qb, kb, vb], axis=-1).astype(BF16)


GDN_CHUNK = 128
SWA_CHUNK = 64


def _tiles(batch, length):
    if length >= 1024:
        return dict(tm=1024, proj_tm=512, gdn_tb=1024, gdn_c=GDN_CHUNK, swa_tb=512, swa_c=SWA_CHUNK)
    return dict(tm=batch * length, proj_tm=length, gdn_tb=GDN_CHUNK, gdn_c=GDN_CHUNK, swa_tb=length, swa_c=length)


def _pad_lanes(v):
    return jnp.pad(v, ((0, 0), (0, GATE_W - v.shape[-1])))[:, None, :]


def _trunk(x, rope_tabs, conv_bufs, s0s, k_caches, v_caches, wts, *, batch, length, tm, proj_tm,
           gdn_tb, gdn_c, swa_tb, swa_c, mask_start):
    (norm_ff1, ff1_in, ff1_out, norm_mix, w_in, w_in_tail, conv_w, alog, dtb, gnorm, sinks, w_out,
     norm_ff2, ff2_in, ff2_out, norm_final) = wts
    cos, sa, sb = rope_tabs
    lpad = -(-length // gdn_c) * gdn_c
    bufs, states, ks, vs = [], [], [], []

    def pad_rows(a):
        return a if lpad == length else jnp.pad(a, ((0, 0), (0, lpad - length), (0, 0)))

    for l in range(DEPTH):
        x = _ffn(x, norm_ff1, ff1_in, ff1_out, layer=l, tm=tm)
        act, z, bg, ag, qb, kb, vb, nbuf = _proj_in(x.reshape(batch, length, D_MODEL), norm_mix, w_in, w_in_tail,
                                                    conv_bufs[l], conv_w, cos, sa, sb, alog, dtb, layer=l, tm=proj_tm)
        if length < gdn_c:
            one = lambda a: pad_rows(a).reshape(1, batch * lpad, a.shape[-1])
            o_a, s_new = _gdn(one(act), one(z), one(bg), one(ag), s0s[l], gnorm[l],
                              tb=batch * lpad, c=gdn_c, n_streams=batch)
            o_a = o_a.reshape(batch, lpad, A_VW)
        else:
            o_a, s_new = _gdn(act, z, bg, ag, s0s[l], gnorm[l], tb=gdn_tb, c=gdn_c)
        o_b, nk, nv = _swa(qb, kb, vb, k_caches[l], v_caches[l], sinks[l],
                           tb=swa_tb, c=swa_c, mask_start=mask_start)
        o_a = o_a[:, :length].reshape(batch * length, A_VW)
        x = _mix_out_ffn(x, o_a, o_b.reshape(batch * length, B_QW), w_out, norm_ff2, ff2_in, ff2_out,
                         norm_final if l == DEPTH - 1 else None, layer=l, tm=tm)
        bufs.append(nbuf)
        states.append(s_new)
        ks.append(nk.reshape(batch, WINDOW, B_KV_HEADS, B_HD))
        vs.append(nv.reshape(batch, WINDOW, B_KV_HEADS, B_HD))
    y = x.reshape(batch, length, D_MODEL)
    return y, jnp.stack(bufs), jnp.stack(states), jnp.stack(ks), jnp.stack(vs)


def kernel(x_prompt, x_sample, cache_conv, state_delta, cache_k, cache_v, norm_ff1, ff1_w_in, ff1_w_out, norm_mix, w_mix_in, conv_w, a_log, dt_bias, gnorm_w, sinks, w_mix_out, norm_ff2, ff2_w_in, ff2_w_out, norm_final):
    bp, lp, _ = x_prompt.shape
    bs, ls, _ = x_sample.shape
    rows = cache_k.shape[2]
    assert rows == WINDOW

    wts = (norm_ff1[:, None, :], ff1_w_in.astype(BF16), ff1_w_out.astype(BF16), norm_mix[:, None, :],
           w_mix_in.astype(BF16), _pack_w_in_tail(w_mix_in), conv_w, _pad_lanes(a_log), _pad_lanes(dt_bias), gnorm_w[:, None, :], sinks,
           w_mix_out.astype(BF16), norm_ff2[:, None, :], ff2_w_in.astype(BF16), ff2_w_out.astype(BF16),
           norm_final[None, :])

    inv = jnp.power(ROPE_THETA, -jnp.arange(0, ROPE_DIM, 2, dtype=F32) / ROPE_DIM)
    rot = ROPE_DIM // 2
    head_row = jnp.concatenate([inv, inv, jnp.zeros((B_HD - ROPE_DIM,), F32)])
    inv_row = jnp.tile(head_row, LANES // B_HD)[None, :]
    d = np.arange(LANES) % B_HD
    ma = jnp.asarray(((d >= rot) & (d < ROPE_DIM)).astype(np.float32))[None, :]
    mb = jnp.asarray((d < rot).astype(np.float32))[None, :]

    zero_buf = jnp.zeros((DEPTH, bp, CONV_W - 1, A_QKV), F32)
    zero_state = jnp.zeros((DEPTH, bp, A_HEADS, A_DK, A_DV), F32)
    zero_kv = jnp.zeros((DEPTH, bp, WINDOW, B_KW), F32)
    tabs_p = _rope_tables(inv_row, ma, mb, length=lp, pos0=0, tm=512)
    y_p, conv_p, delta_p, k_p, v_p = _trunk(
        x_prompt.reshape(bp * lp, D_MODEL), tabs_p, zero_buf, zero_state, zero_kv, zero_kv, wts,
        batch=bp, length=lp, mask_start=True, **_tiles(bp, lp))

    tabs_s = _rope_tables(inv_row, ma, mb, length=ls, pos0=PAST_LEN, tm=ls)
    y_s, conv_s, delta_s, k_s, v_s = _trunk(
        x_sample.reshape(bs * ls, D_MODEL), tabs_s, cache_conv, state_delta,
        cache_k.reshape(DEPTH, bs, WINDOW, B_KW), cache_v.reshape(DEPTH, bs, WINDOW, B_KW), wts,
        batch=bs, length=ls, mask_start=False, **_tiles(bs, ls))
    return (y_p, y_s, conv_p, delta_p, k_p, v_p, conv_s, delta_s, k_s, v_s)
```

```python
import functools

import numpy as np
import jax
import jax.numpy as jnp
from jax import lax
from jax.experimental import pallas as pl
from jax.experimental.pallas import tpu as pltpu

F32 = jnp.float32
BF16 = jnp.bfloat16

D_MODEL = 1024
DEPTH = 4
PAST_LEN = 4096
EPS = 1e-6
A_HEADS = 4
A_DK = 128
A_DV = 128
A_QK = A_HEADS * A_DK
A_VW = A_HEADS * A_DV
A_QKV = 2 * A_QK + A_VW
CONV_W = 4
B_HEADS = 8
B_KV_HEADS = 2
B_HD = 64
B_QW = B_HEADS * B_HD
B_KW = B_KV_HEADS * B_HD
WINDOW = 128
ROPE_DIM = B_HD // 4
ROPE_THETA = 500000.0
D_MIX = A_VW + B_QW
D_FF = 2816

LANES = 128
SUBLANES = 8
GATE_W = LANES
VMEM_LIMIT = 56 * 1024 * 1024


def _params(sem, vmem=VMEM_LIMIT):
    return pltpu.CompilerParams(dimension_semantics=sem, vmem_limit_bytes=vmem)


def _resident(shape, layer=None):
    nd = len(shape)
    if layer is None:
        return pl.BlockSpec(shape, lambda *_: (0,) * nd, pipeline_mode=pl.Buffered(1))
    return pl.BlockSpec((None,) + tuple(shape), lambda *_: (layer,) + (0,) * nd, pipeline_mode=pl.Buffered(1))


def _rms(x, w):
    return x * lax.rsqrt(jnp.mean(x * x, axis=-1, keepdims=True) + EPS) * w


def _silu(x):
    return x * jax.nn.sigmoid(x)


def _mm(a, b):
    return jnp.dot(a.astype(BF16), b.astype(BF16), preferred_element_type=F32)


def _swiglu_residual(x, nw_ref, win_ref, wout_ref, fc):
    h = _rms(x, nw_ref[...]).astype(BF16)
    acc = None
    for c in range(0, D_FF, fc):
        gate = jnp.dot(h, win_ref[:, c:c + fc], preferred_element_type=F32)
        up = jnp.dot(h, win_ref[:, D_FF + c:D_FF + c + fc], preferred_element_type=F32)
        act = (_silu(gate) * up).astype(BF16)
        part = jnp.dot(act, wout_ref[c:c + fc, :], preferred_element_type=F32)
        acc = part if acc is None else acc + part
    return x + 0.5 * acc


FFN_ROWS = 512


def _row_groups(n):
    return [(r, min(FFN_ROWS, n - r)) for r in range(0, n, FFN_ROWS)]


def _ffn_body(x_ref, nw_ref, win_ref, wout_ref, o_ref, *, fc):
    for r, n in _row_groups(x_ref.shape[0]):
        o_ref[r:r + n, :] = _swiglu_residual(x_ref[r:r + n, :], nw_ref, win_ref, wout_ref, fc)


def _ffn(x, nw, w_in, w_out, *, layer, tm, fc=2816):
    m = x.shape[0]
    return pl.pallas_call(
        functools.partial(_ffn_body, fc=fc),
        grid=(m // tm,),
        in_specs=[pl.BlockSpec((tm, D_MODEL), lambda i: (i, 0)),
                  _resident((1, D_MODEL), layer),
                  _resident((D_MODEL, 2 * D_FF), layer),
                  _resident((D_FF, D_MODEL), layer)],
        out_specs=pl.BlockSpec((tm, D_MODEL), lambda i: (i, 0)),
        out_shape=jax.ShapeDtypeStruct((m, D_MODEL), F32),
        compiler_params=_params(("parallel",)),
        name="ffn",
    )(x, nw, w_in, w_out)


def _mix_out_ffn_body(x_ref, oa_ref, ob_ref, wo_ref, nw_ref, win_ref, wout_ref, *rest, fc, final):
    o_ref = rest[-1]
    for r, n in _row_groups(x_ref.shape[0]):
        x = (x_ref[r:r + n, :]
             + jnp.dot(oa_ref[r:r + n, :].astype(BF16), wo_ref[0:A_VW, :], preferred_element_type=F32)
             + jnp.dot(ob_ref[r:r + n, :].astype(BF16), wo_ref[A_VW:D_MIX, :], preferred_element_type=F32))
        y = _swiglu_residual(x, nw_ref, win_ref, wout_ref, fc)
        o_ref[r:r + n, :] = _rms(y, rest[0][...]) if final else y


def _mix_out_ffn(x, oa, ob, wo, nw, w_in, w_out, norm_final=None, *, layer, tm, fc=2816):
    m = x.shape[0]
    final = norm_final is not None
    row = lambda width: pl.BlockSpec((tm, width), lambda i: (i, 0))
    in_specs = [row(D_MODEL), row(A_VW), row(B_QW), _resident((D_MIX, D_MODEL), layer),
                _resident((1, D_MODEL), layer), _resident((D_MODEL, 2 * D_FF), layer),
                _resident((D_FF, D_MODEL), layer)]
    args = [x, oa, ob, wo, nw, w_in, w_out]
    if final:
        in_specs.append(_resident((1, D_MODEL)))
        args.append(norm_final)
    return pl.pallas_call(
        functools.partial(_mix_out_ffn_body, fc=fc, final=final),
        grid=(m // tm,),
        in_specs=in_specs,
        out_specs=row(D_MODEL),
        out_shape=jax.ShapeDtypeStruct((m, D_MODEL), F32),
        compiler_params=_params(("parallel",)),
        name="mix_out_ffn",
    )(*args)


O_TAIL = A_QKV + A_VW
_TAIL_GROUPS = (GATE_W, GATE_W, B_QW, B_KW, B_KW)
D_TAIL = sum(_TAIL_GROUPS)
HALO = CONV_W - 1
HALO_BASE = SUBLANES


def _rope(x, cos, sa, sb):
    rot = ROPE_DIM // 2
    return x * cos + pltpu.roll(x, rot, 1) * sa + pltpu.roll(x, LANES - rot, 1) * sb


def _proj_in_body(x_ref, nw_ref, w_ref, wt_ref, cbuf_ref, cw_ref, cos_ref, sa_ref, sb_ref, alog_ref, dtb_ref,
                  act_ref, z_ref, bg_ref, ag_ref, qb_ref, kb_ref, vb_ref, nbuf_ref, xp_ref, *, tm):
    j = pl.program_id(1)

    @pl.when(j == 0)
    def _():
        xp_ref[HALO_BASE - HALO:HALO_BASE, :] = cbuf_ref[...]

    h = _rms(x_ref[...], nw_ref[...]).astype(BF16)
    qkv = jnp.dot(h, w_ref[:, 0:A_QKV], preferred_element_type=F32)
    xp_ref[HALO_BASE:HALO_BASE + tm, :] = qkv
    z_ref[...] = jnp.dot(h, w_ref[:, A_QKV:O_TAIL], preferred_element_type=F32)
    offs = np.cumsum((0,) + _TAIL_GROUPS).tolist()
    bg, ag, qb, kb, vb = [jnp.dot(h, wt_ref[:, offs[i]:offs[i + 1]], preferred_element_type=F32)
                          for i in range(len(_TAIL_GROUPS))]
    bg_ref[...] = jax.nn.sigmoid(bg)
    a_in = ag + dtb_ref[...]
    softplus = jnp.maximum(a_in, 0.0) + jnp.log1p(jnp.exp(-jnp.abs(a_in)))
    ag_ref[...] = -jnp.exp(alog_ref[...]) * softplus
    vb_ref[...] = vb
    cos, sa, sb = cos_ref[...], sa_ref[...], sb_ref[...]
    kb_ref[...] = _rope(kb, cos, sa, sb)
    for s in range(B_QW // LANES):
        qb_ref[:, s * LANES:(s + 1) * LANES] = (
            _rope(qb[:, s * LANES:(s + 1) * LANES], cos, sa, sb) * (B_HD ** -0.5)).astype(BF16)

    cw = cw_ref[...]
    conv = qkv * cw[0:1, :]
    for t in range(1, CONV_W):
        conv = pltpu.roll(conv, 1, 0) + qkv * cw[t:t + 1, :]
    act_ref[...] = _silu(conv)
    lo = HALO_BASE - HALO
    head = xp_ref[lo:lo + SUBLANES, :] * cw[0:1, :]
    for t in range(1, CONV_W):
        head = head + xp_ref[lo + t:lo + t + SUBLANES, :] * cw[t:t + 1, :]
    act_ref[0:SUBLANES, :] = _silu(head)
    for hd in range(2 * A_HEADS):
        cols = slice(hd * A_DK, (hd + 1) * A_DK)
        t = act_ref[:, cols]
        t = t * lax.rsqrt(jnp.sum(t * t, axis=-1, keepdims=True) + EPS)
        act_ref[:, cols] = t * (A_DK ** -0.5) if hd < A_HEADS else t

    @pl.when(j == pl.num_programs(1) - 1)
    def _():
        nbuf_ref[...] = xp_ref[HALO_BASE + tm - HALO:HALO_BASE + tm, :]

    xp_ref[HALO_BASE - HALO:HALO_BASE, :] = xp_ref[HALO_BASE + tm - HALO:HALO_BASE + tm, :]


def _proj_in(x, nw, w, w_tail, cbuf, cw, cos, sa, sb, alog, dtb, *, layer, tm):
    b, l, _ = x.shape
    blk = lambda width: pl.BlockSpec((None, tm, width), lambda i, j: (i, j, 0))
    buf = pl.BlockSpec((None, HALO, A_QKV), lambda i, j: (i, 0, 0))
    tab = pl.BlockSpec((tm, LANES), lambda i, j: (j, 0))
    widths = [A_QKV, A_VW] + list(_TAIL_GROUPS)
    dtypes = [F32, F32, F32, F32, BF16, F32, F32]
    return pl.pallas_call(
        functools.partial(_proj_in_body, tm=tm),
        grid=(b, l // tm),
        in_specs=[blk(D_MODEL), _resident((1, D_MODEL), layer), _resident((D_MODEL, O_TAIL), layer),
                  _resident((D_MODEL, D_TAIL), layer), buf, _resident((CONV_W, A_QKV), layer), tab, tab, tab,
                  _resident((1, GATE_W), layer), _resident((1, GATE_W), layer)],
        out_specs=[blk(width) for width in widths] + [buf],
        out_shape=[jax.ShapeDtypeStruct((b, l, width), dt) for width, dt in zip(widths, dtypes)]
                  + [jax.ShapeDtypeStruct((b, HALO, A_QKV), F32)],
        scratch_shapes=[pltpu.VMEM((HALO_BASE + tm, A_QKV), F32)],
        compiler_params=_params(("parallel", "arbitrary")),
        name="proj_in",
    )(x, nw, w, w_tail, cbuf, cw, cos, sa, sb, alog, dtb)


def _chunk_cumsum(g, tril):
    g1 = g.astype(BF16)
    r1 = g - g1.astype(F32)
    g2 = r1.astype(BF16)
    g3 = (r1 - g2.astype(F32)).astype(BF16)
    parts = jnp.dot(tril, jnp.concatenate([g1, g2, g3], axis=1), preferred_element_type=F32)
    return parts[:, 0:LANES] + parts[:, LANES:2 * LANES] + parts[:, 2 * LANES:3 * LANES]


INV_BLOCK = 32


def _unit_lower_inverses_minus_eye(ls, c):
    bs = min(INV_BLOCK, c)
    row = lax.broadcasted_iota(jnp.int32, (c, c), 0)
    col = lax.broadcasted_iota(jnp.int32, (c, c), 1)
    same = lambda b: (row // b) == (col // b)
    diag = [jnp.where(same(bs), l, 0.0) for l in ls] if bs < c else ls
    p = [-d for d in diag]
    m = [_mm(d, d) for d in diag]
    for _ in range(int(np.log2(bs)) - 2):
        pm = [_mm(pi, mi) for pi, mi in zip(p, m)]
        p = [pi + mi + pmi for pi, mi, pmi in zip(p, m, pm)]
        m = [_mm(mi, mi) for mi in m]
    pm = [_mm(pi, mi) for pi, mi in zip(p, m)]
    p = [pi + mi + pmi for pi, mi, pmi in zip(p, m, pm)]
    b = bs
    while b < c:
        sel = same(2 * b) & jnp.logical_not(same(b))
        off = [jnp.where(sel, l, 0.0) for l in ls]
        x = [oi + _mm(pi, oi) for pi, oi in zip(p, off)]
        p = [pi - (xi + _mm(xi, pi)) for pi, xi in zip(p, x)]
        b *= 2
    return p


def _gdn_body(act_ref, z_ref, beta_ref, g_ref, s0_ref, gw_ref,
              o_ref, snew_ref, s_ref, u_ref, wq_ref, qk_ref, kdec_ref, *, tb, c, n_streams):
    t = pl.program_id(1)
    seq = tb // n_streams

    @pl.when(t == 0)
    def _():
        s_ref[...] = s0_ref[...]

    beta_all = beta_ref[...]
    g_all = g_ref[...]

    row = lax.broadcasted_iota(jnp.int32, (c, c), 0)
    col = lax.broadcasted_iota(jnp.int32, (c, c), 1)
    causal = row >= col
    strict = row > col
    tril = jnp.where(causal, 1.0, 0.0).astype(BF16)
    gw = gw_ref[...]
    n_chunks = tb // c

    g_tots = []
    lmats, rhss = [], []
    for ci in range(n_chunks):
        r0 = ci * c
        beta = beta_all[r0:r0 + c, :]
        g_cum = _chunk_cumsum(g_all[r0:r0 + c, :], tril)
        g_cum_t = g_cum.T
        g_last = g_cum[c - 1:c, :]
        e_g = jnp.exp(g_cum)
        e_gl = jnp.exp(g_last - g_cum)
        g_tots.append(jnp.exp(g_last))
        for h in range(A_HEADS):
            i = ci * A_HEADS + h
            q = act_ref[r0:r0 + c, h * A_DK:(h + 1) * A_DK]
            k = act_ref[r0:r0 + c, A_QK + h * A_DK:A_QK + (h + 1) * A_DK]
            v = act_ref[r0:r0 + c, 2 * A_QK + h * A_DV:2 * A_QK + (h + 1) * A_DV]
            b_col = beta[:, h:h + 1]
            eg_col = e_g[:, h:h + 1]
            decay = jnp.exp(jnp.where(causal, g_cum[:, h:h + 1] - g_cum_t[h:h + 1, :], -jnp.inf))
            k16 = k.astype(BF16)
            qk_kk = lax.dot_general(jnp.concatenate([q.astype(BF16), k16], axis=0), k16,
                                    (((1,), (1,)), ((), ())), preferred_element_type=F32)
            qk_ref[i] = (qk_kk[0:c, :] * decay).astype(BF16)
            lmats.append(jnp.where(strict, qk_kk[c:2 * c, :] * decay * b_col, 0.0))
            rhss.append(jnp.concatenate([v * b_col, k * (b_col * eg_col)], axis=1))
            wq_ref[i, c:2 * c, :] = (q * eg_col).astype(BF16)
            kdec_ref[i] = (k * e_gl[:, h:h + 1]).astype(BF16)
    tinvs = _unit_lower_inverses_minus_eye(lmats, c)
    for i, (tinv, rhs) in enumerate(zip(tinvs, rhss)):
        sol = rhs + _mm(tinv, rhs)
        u_ref[i] = sol[:, 0:A_DV]
        wq_ref[i, 0:c, :] = sol[:, A_DV:A_DV + A_DK].astype(BF16)

    for step in range(seq // c):
        lanes = [(sq, h, sq * (seq // c) + step) for sq in range(n_streams) for h in range(A_HEADS)]
        idx = [ci * A_HEADS + h for sq, h, ci in lanes]
        s_old = [s_ref[sq, h] for sq, h, ci in lanes]
        wq_s = [jnp.dot(wq_ref[i], st.astype(BF16), preferred_element_type=F32) for i, st in zip(idx, s_old)]
        v_new = [u_ref[i] - x[0:c, :] for i, x in zip(idx, wq_s)]
        v16 = [x.astype(BF16) for x in v_new]
        for k, (sq, h, ci) in enumerate(lanes):
            s_ref[sq, h] = s_old[k] * g_tots[ci][:, h:h + 1] + lax.dot_general(
                kdec_ref[idx[k]], v16[k], (((0,), (0,)), ((), ())), preferred_element_type=F32)
        for k, (sq, h, ci) in enumerate(lanes):
            r0 = ci * c
            o = wq_s[k][c:2 * c, :] + jnp.dot(qk_ref[idx[k]], v16[k], preferred_element_type=F32)
            o = o * lax.rsqrt(jnp.mean(o * o, axis=-1, keepdims=True) + EPS) * gw
            o_ref[r0:r0 + c, h * A_DV:(h + 1) * A_DV] = o * _silu(z_ref[r0:r0 + c, h * A_DV:(h + 1) * A_DV])

    @pl.when(t == pl.num_programs(1) - 1)
    def _():
        snew_ref[...] = s_ref[...]


def _gdn(act, z, beta, g, s0, gw, *, tb, c, n_streams=1):
    b, l, _ = act.shape
    nt = l // tb
    assert n_streams == 1 or nt == 1
    n_items = (tb // c) * A_HEADS
    blk = lambda width: pl.BlockSpec((None, tb, width), lambda i, j: (i, j, 0))
    state = pl.BlockSpec((n_streams, A_HEADS, A_DK, A_DV), lambda i, j: (i, 0, 0, 0))
    lane_row = pl.BlockSpec((1, LANES), lambda i, j: (0, 0))
    return pl.pallas_call(
        functools.partial(_gdn_body, tb=tb, c=c, n_streams=n_streams),
        grid=(b, nt),
        in_specs=[blk(A_QKV), blk(A_VW), blk(GATE_W), blk(GATE_W), state, lane_row],
        out_specs=[blk(A_VW), state],
        out_shape=[jax.ShapeDtypeStruct((b, l, A_VW), F32),
                   jax.ShapeDtypeStruct((b * n_streams, A_HEADS, A_DK, A_DV), F32)],
        scratch_shapes=[pltpu.VMEM((n_streams, A_HEADS, A_DK, A_DV), F32),
                        pltpu.VMEM((n_items, c, A_DV), F32),
                        pltpu.VMEM((n_items, 2 * c, A_DK), BF16),
                        pltpu.VMEM((n_items, c, c), BF16),
                        pltpu.VMEM((n_items, c, A_DK), BF16)],
        compiler_params=_params(("parallel", "arbitrary")),
        name="gdn",
    )(act, z, beta, g, s0, gw)


def _rope_table_body(inv_ref, ma_ref, mb_ref, cos_ref, sa_ref, sb_ref, c0_ref, s0_ref, *, tm, pos0):
    i = pl.program_id(0)
    inv = inv_ref[...]

    @pl.when(i == 0)
    def _():
        ang = lax.broadcasted_iota(jnp.int32, (tm, LANES), 0).astype(F32) * inv
        c0_ref[...] = jnp.cos(ang)
        s0_ref[...] = jnp.sin(ang)

    base = (pos0 + i * tm).astype(F32) * inv
    cb = jnp.cos(base)
    sn = jnp.sin(base)
    c0 = c0_ref[...]
    s0 = s0_ref[...]
    sin = s0 * cb + c0 * sn
    cos_ref[...] = c0 * cb - s0 * sn
    sa_ref[...] = sin * ma_ref[...]
    sb_ref[...] = -sin * mb_ref[...]


def _rope_tables(inv_row, ma, mb, *, length, pos0, tm):
    row = pl.BlockSpec((1, LANES), lambda i: (0, 0))
    out = pl.BlockSpec((tm, LANES), lambda i: (i, 0))
    return pl.pallas_call(
        functools.partial(_rope_table_body, tm=tm, pos0=pos0),
        grid=(length // tm,),
        in_specs=[row, row, row],
        out_specs=[out, out, out],
        out_shape=[jax.ShapeDtypeStruct((length, LANES), F32)] * 3,
        scratch_shapes=[pltpu.VMEM((tm, LANES), F32), pltpu.VMEM((tm, LANES), F32)],
        compiler_params=_params(("arbitrary",)),
        name="rope_tables",
    )(inv_row, ma, mb)


SWA_LOCKSTEP = 4


def _swa_body(qs_ref, kb_ref, vb_ref, kc_ref, vc_ref, sinks_ref,
              o_ref, newk_ref, newv_ref, kx_ref, vx_ref, kvar_ref, vvar_ref, *, tb, c, mask_start):
    t = pl.program_id(1)
    half = B_HD
    w = WINDOW + c

    @pl.when(t == 0)
    def _():
        kx_ref[0:WINDOW, :] = kc_ref[...]
        vx_ref[0:WINDOW, :] = vc_ref[...]

    kx_ref[WINDOW:WINDOW + tb, :] = kb_ref[...]
    vx_ref[WINDOW:WINDOW + tb, :] = vb_ref[...]

    lo = lax.broadcasted_iota(jnp.int32, (WINDOW + tb, LANES), 1) < half
    for src, dst in ((kx_ref, kvar_ref), (vx_ref, vvar_ref)):
        full = src[...]
        swapped = pltpu.roll(full, half, 1)
        dst[0] = jnp.where(lo, full, 0.0).astype(BF16)
        dst[1] = jnp.where(lo, 0.0, swapped).astype(BF16)
        dst[2] = jnp.where(lo, swapped, 0.0).astype(BF16)
        dst[3] = jnp.where(lo, 0.0, full).astype(BF16)

    first_rows = lax.broadcasted_iota(jnp.int32, (2 * c, 1), 0) < c
    kcol = lax.broadcasted_iota(jnp.int32, (2 * c, w), 1)

    sks = [jnp.where(first_rows, sinks_ref[4 * g + hh], sinks_ref[4 * g + 2 + hh])
           for g in range(B_KV_HEADS) for hh in range(2)]
    n_chunks = tb // c
    for i0 in range(0, n_chunks, SWA_LOCKSTEP):
        items = [(i, g, hh) for i in range(i0, min(i0 + SWA_LOCKSTEP, n_chunks))
                 for g in range(B_KV_HEADS) for hh in range(2)]
        lhs = {(i, g): jnp.concatenate([qs_ref[i * c:(i + 1) * c, (2 * g) * LANES:(2 * g + 1) * LANES],
                                        qs_ref[i * c:(i + 1) * c, (2 * g + 1) * LANES:(2 * g + 2) * LANES]], axis=0)
               for i, g, hh in items if hh == 0}
        s = [lax.dot_general(lhs[i, g], kvar_ref[2 * g + hh, i * c:i * c + w, :],
                             (((1,), (1,)), ((), ())), preferred_element_type=F32) for i, g, hh in items]
        if mask_start:
            s = [jnp.where(t * tb + i * c - WINDOW + kcol >= 0, si, -jnp.inf) for si, (i, g, hh) in zip(s, items)]
        m = [jnp.maximum(jnp.max(si, axis=-1, keepdims=True), sks[2 * g + hh]) for si, (i, g, hh) in zip(s, items)]
        p = [jnp.exp(si - mi) for si, mi in zip(s, m)]
        den = [jnp.sum(pi, axis=-1, keepdims=True) + jnp.exp(sks[2 * g + hh] - mi)
               for pi, mi, (i, g, hh) in zip(p, m, items)]
        pv = [jnp.dot(pi.astype(BF16), vvar_ref[2 * g + hh, i * c:i * c + w, :], preferred_element_type=F32) / di
              for pi, di, (i, g, hh) in zip(p, den, items)]
        for j in range(0, len(items), 2):
            i, g, _ = items[j]
            o = pv[j] + pv[j + 1]
            o_ref[i * c:(i + 1) * c, (2 * g) * LANES:(2 * g + 1) * LANES] = o[0:c, :]
            o_ref[i * c:(i + 1) * c, (2 * g + 1) * LANES:(2 * g + 2) * LANES] = o[c:2 * c, :]

    @pl.when(t == pl.num_programs(1) - 1)
    def _():
        newk_ref[...] = kx_ref[tb:tb + WINDOW, :]
        newv_ref[...] = vx_ref[tb:tb + WINDOW, :]

    if tb >= WINDOW:
        kx_ref[0:WINDOW, :] = kx_ref[tb:tb + WINDOW, :]
        vx_ref[0:WINDOW, :] = vx_ref[tb:tb + WINDOW, :]


def _swa(qb, kb, vb, kc, vc, sinks, *, tb, c, mask_start):
    b, l, _ = qb.shape
    nt = l // tb
    assert tb >= WINDOW or nt == 1
    blk = lambda width: pl.BlockSpec((None, tb, width), lambda i, j: (i, j, 0))
    cache = pl.BlockSpec((None, WINDOW, B_KW), lambda i, j: (i, 0, 0))
    return pl.pallas_call(
        functools.partial(_swa_body, tb=tb, c=c, mask_start=mask_start),
        grid=(b, nt),
        in_specs=[blk(B_QW), blk(B_KW), blk(B_KW), cache, cache, pl.BlockSpec(memory_space=pltpu.SMEM)],
        out_specs=[blk(B_QW), cache, cache],
        out_shape=[jax.ShapeDtypeStruct((b, l, B_QW), F32),
                   jax.ShapeDtypeStruct((b, WINDOW, B_KW), F32),
                   jax.ShapeDtypeStruct((b, WINDOW, B_KW), F32)],
        scratch_shapes=[pltpu.VMEM((WINDOW + tb, B_KW), F32),
                        pltpu.VMEM((WINDOW + tb, B_KW), F32),
                        pltpu.VMEM((4, WINDOW + tb, LANES), BF16),
                        pltpu.VMEM((4, WINDOW + tb, LANES), BF16)],
        compiler_params=_params(("parallel", "arbitrary")),
        name="swa",
    )(qb, kb, vb, kc, vc, sinks)


def _pack_w_in_tail(w):
    offs = np.cumsum([O_TAIL, A_HEADS, A_HEADS, B_QW, B_KW, B_KW]).tolist()
    bg, ag, qb, kb, vb = [w[..., offs[i]:offs[i + 1]] for i in range(5)]
    pad = lambda g: jnp.pad(g, ((0, 0), (0, 0), (0, GATE_W - g.shape[-1])))
    return jnp.concatenate([pad(bg), pad(ag), qb, kb, vb], axis=-1).astype(BF16)


GDN_CHUNK = 128
SWA_CHUNK = 64


def _tiles(batch, length):
    if length >= 1024:
        return dict(tm=1024, proj_tm=512, gdn_tb=1024, gdn_c=GDN_CHUNK, swa_tb=512, swa_c=SWA_CHUNK)
    return dict(tm=batch * length, proj_tm=length, gdn_tb=GDN_CHUNK, gdn_c=GDN_CHUNK, swa_tb=length, swa_c=length)


def _pad_lanes(v):
    return jnp.pad(v, ((0, 0), (0, GATE_W - v.shape[-1])))[:, None, :]


def _trunk(x, rope_tabs, conv_bufs, s0s, k_caches, v_caches, wts, *, batch, length, tm, proj_tm,
           gdn_tb, gdn_c, swa_tb, swa_c, mask_start):
    (norm_ff1, ff1_in, ff1_out, norm_mix, w_in, w_in_tail, conv_w, alog, dtb, gnorm, sinks, w_out,
     norm_ff2, ff2_in, ff2_out, norm_final) = wts
    cos, sa, sb = rope_tabs
    lpad = -(-length // gdn_c) * gdn_c
    bufs, states, ks, vs = [], [], [], []

    def pad_rows(a):
        return a if lpad == length else jnp.pad(a, ((0, 0), (0, lpad - length), (0, 0)))

    for l in range(DEPTH):
        x = _ffn(x, norm_ff1, ff1_in, ff1_out, layer=l, tm=tm)
        act, z, bg, ag, qb, kb, vb, nbuf = _proj_in(x.reshape(batch, length, D_MODEL), norm_mix, w_in, w_in_tail,
                                                    conv_bufs[l], conv_w, cos, sa, sb, alog, dtb, layer=l, tm=proj_tm)
        if length < gdn_c:
            one = lambda a: pad_rows(a).reshape(1, batch * lpad, a.shape[-1])
            o_a, s_new = _gdn(one(act), one(z), one(bg), one(ag), s0s[l], gnorm[l],
                              tb=batch * lpad, c=gdn_c, n_streams=batch)
            o_a = o_a.reshape(batch, lpad, A_VW)
        else:
            o_a, s_new = _gdn(act, z, bg, ag, s0s[l], gnorm[l], tb=gdn_tb, c=gdn_c)
        o_b, nk, nv = _swa(qb, kb, vb, k_caches[l], v_caches[l], sinks[l],
                           tb=swa_tb, c=swa_c, mask_start=mask_start)
        o_a = o_a[:, :length].reshape(batch * length, A_VW)
        x = _mix_out_ffn(x, o_a, o_b.reshape(batch * length, B_QW), w_out, norm_ff2, ff2_in, ff2_out,
                         norm_final if l == DEPTH - 1 else None, layer=l, tm=tm)
        bufs.append(nbuf)
        states.append(s_new)
        ks.append(nk.reshape(batch, WINDOW, B_KV_HEADS, B_HD))
        vs.append(nv.reshape(batch, WINDOW, B_KV_HEADS, B_HD))
    y = x.reshape(batch, length, D_MODEL)
    return y, jnp.stack(bufs), jnp.stack(states), jnp.stack(ks), jnp.stack(vs)


def kernel(x_prompt, x_sample, cache_conv, state_delta, cache_k, cache_v, norm_ff1, ff1_w_in, ff1_w_out, norm_mix, w_mix_in, conv_w, a_log, dt_bias, gnorm_w, sinks, w_mix_out, norm_ff2, ff2_w_in, ff2_w_out, norm_final):
    bp, lp, _ = x_prompt.shape
    bs, ls, _ = x_sample.shape
    rows = cache_k.shape[2]
    assert rows == WINDOW

    wts = (norm_ff1[:, None, :], ff1_w_in.astype(BF16), ff1_w_out.astype(BF16), norm_mix[:, None, :],
           w_mix_in[..., :O_TAIL].astype(BF16), _pack_w_in_tail(w_mix_in), conv_w, _pad_lanes(a_log), _pad_lanes(dt_bias), gnorm_w[:, None, :], sinks,
           w_mix_out.astype(BF16), norm_ff2[:, None, :], ff2_w_in.astype(BF16), ff2_w_out.astype(BF16),
           norm_final[None, :])

    inv = jnp.power(ROPE_THETA, -jnp.arange(0, ROPE_DIM, 2, dtype=F32) / ROPE_DIM)
    rot = ROPE_DIM // 2
    head_row = jnp.concatenate([inv, inv, jnp.zeros((B_HD - ROPE_DIM,), F32)])
    inv_row = jnp.tile(head_row, LANES // B_HD)[None, :]
    d = np.arange(LANES) % B_HD
    ma = jnp.asarray(((d >= rot) & (d < ROPE_DIM)).astype(np.float32))[None, :]
    mb = jnp.asarray((d < rot).astype(np.float32))[None, :]

    zero_buf = jnp.zeros((DEPTH, bp, CONV_W - 1, A_QKV), F32)
    zero_state = jnp.zeros((DEPTH, bp, A_HEADS, A_DK, A_DV), F32)
    zero_kv = jnp.zeros((DEPTH, bp, WINDOW, B_KW), F32)
    tabs_p = _rope_tables(inv_row, ma, mb, length=lp, pos0=0, tm=512)
    y_p, conv_p, delta_p, k_p, v_p = _trunk(
        x_prompt.reshape(bp * lp, D_MODEL), tabs_p, zero_buf, zero_state, zero_kv, zero_kv, wts,
        batch=bp, length=lp, mask_start=True, **_tiles(bp, lp))

    tabs_s = _rope_tables(inv_row, ma, mb, length=ls, pos0=PAST_LEN, tm=ls)
    y_s, conv_s, delta_s, k_s, v_s = _trunk(
        x_sample.reshape(bs * ls, D_MODEL), tabs_s, cache_conv, state_delta,
        cache_k.reshape(DEPTH, bs, WINDOW, B_KW), cache_v.reshape(DEPTH, bs, WINDOW, B_KW), wts,
        batch=bs, length=ls, mask_start=False, **_tiles(bs, ls))
    return (y_p, y_s, conv_p, delta_p, k_p, v_p, conv_s, delta_s, k_s, v_s)
```

```python
import functools

import numpy as np
import jax
import jax.numpy as jnp
from jax import lax
from jax.experimental import pallas as pl
from jax.experimental.pallas import tpu as pltpu

F32 = jnp.float32
BF16 = jnp.bfloat16

D_MODEL = 1024
DEPTH = 4
PAST_LEN = 4096
EPS = 1e-6
A_HEADS = 4
A_DK = 128
A_DV = 128
A_QK = A_HEADS * A_DK
A_VW = A_HEADS * A_DV
A_QKV = 2 * A_QK + A_VW
CONV_W = 4
B_HEADS = 8
B_KV_HEADS = 2
B_HD = 64
B_QW = B_HEADS * B_HD
B_KW = B_KV_HEADS * B_HD
WINDOW = 128
ROPE_DIM = B_HD // 4
ROPE_THETA = 500000.0
D_MIX = A_VW + B_QW
D_FF = 2816

LANES = 128
SUBLANES = 8
GATE_W = LANES
VMEM_LIMIT = 56 * 1024 * 1024


def _params(sem, vmem=VMEM_LIMIT):
    return pltpu.CompilerParams(dimension_semantics=sem, vmem_limit_bytes=vmem)


def _resident(shape, layer=None):
    nd = len(shape)
    if layer is None:
        return pl.BlockSpec(shape, lambda *_: (0,) * nd, pipeline_mode=pl.Buffered(1))
    return pl.BlockSpec((None,) + tuple(shape), lambda *_: (layer,) + (0,) * nd, pipeline_mode=pl.Buffered(1))


def _rms(x, w):
    return x * lax.rsqrt(jnp.mean(x * x, axis=-1, keepdims=True) + EPS) * w


def _silu(x):
    return x * jax.nn.sigmoid(x)


def _mm(a, b):
    return jnp.dot(a.astype(BF16), b.astype(BF16), preferred_element_type=F32)


def _swiglu_residual(x, nw_ref, win_ref, wout_ref, fc):
    h = _rms(x, nw_ref[...]).astype(BF16)
    acc = None
    for c in range(0, D_FF, fc):
        gate = jnp.dot(h, win_ref[:, c:c + fc], preferred_element_type=F32)
        up = jnp.dot(h, win_ref[:, D_FF + c:D_FF + c + fc], preferred_element_type=F32)
        act = (_silu(gate) * up).astype(BF16)
        part = jnp.dot(act, wout_ref[c:c + fc, :], preferred_element_type=F32)
        acc = part if acc is None else acc + part
    return x + 0.5 * acc


FFN_ROWS = 512


def _row_groups(n):
    return [(r, min(FFN_ROWS, n - r)) for r in range(0, n, FFN_ROWS)]


def _ffn_body(x_ref, nw_ref, win_ref, wout_ref, o_ref, *, fc):
    for r, n in _row_groups(x_ref.shape[0]):
        o_ref[r:r + n, :] = _swiglu_residual(x_ref[r:r + n, :], nw_ref, win_ref, wout_ref, fc)


def _ffn(x, nw, w_in, w_out, *, layer, tm, fc=2816):
    m = x.shape[0]
    return pl.pallas_call(
        functools.partial(_ffn_body, fc=fc),
        grid=(m // tm,),
        in_specs=[pl.BlockSpec((tm, D_MODEL), lambda i: (i, 0)),
                  _resident((1, D_MODEL), layer),
                  _resident((D_MODEL, 2 * D_FF), layer),
                  _resident((D_FF, D_MODEL), layer)],
        out_specs=pl.BlockSpec((tm, D_MODEL), lambda i: (i, 0)),
        out_shape=jax.ShapeDtypeStruct((m, D_MODEL), F32),
        compiler_params=_params(("parallel",)),
        name="ffn",
    )(x, nw, w_in, w_out)


def _mix_out_ffn_body(x_ref, oa_ref, ob_ref, wo_ref, nw_ref, win_ref, wout_ref, *rest, fc, final):
    o_ref = rest[-1]
    for r, n in _row_groups(x_ref.shape[0]):
        x = (x_ref[r:r + n, :]
             + jnp.dot(oa_ref[r:r + n, :].astype(BF16), wo_ref[0:A_VW, :], preferred_element_type=F32)
             + jnp.dot(ob_ref[r:r + n, :].astype(BF16), wo_ref[A_VW:D_MIX, :], preferred_element_type=F32))
        y = _swiglu_residual(x, nw_ref, win_ref, wout_ref, fc)
        o_ref[r:r + n, :] = _rms(y, rest[0][...]) if final else y


def _mix_out_ffn(x, oa, ob, wo, nw, w_in, w_out, norm_final=None, *, layer, tm, fc=2816):
    m = x.shape[0]
    final = norm_final is not None
    row = lambda width: pl.BlockSpec((tm, width), lambda i: (i, 0))
    in_specs = [row(D_MODEL), row(A_VW), row(B_QW), _resident((D_MIX, D_MODEL), layer),
                _resident((1, D_MODEL), layer), _resident((D_MODEL, 2 * D_FF), layer),
                _resident((D_FF, D_MODEL), layer)]
    args = [x, oa, ob, wo, nw, w_in, w_out]
    if final:
        in_specs.append(_resident((1, D_MODEL)))
        args.append(norm_final)
    return pl.pallas_call(
        functools.partial(_mix_out_ffn_body, fc=fc, final=final),
        grid=(m // tm,),
        in_specs=in_specs,
        out_specs=row(D_MODEL),
        out_shape=jax.ShapeDtypeStruct((m, D_MODEL), F32),
        compiler_params=_params(("parallel",)),
        name="mix_out_ffn",
    )(*args)


O_TAIL = A_QKV + A_VW
_TAIL_GROUPS = (GATE_W, GATE_W, B_QW, B_KW, B_KW)
D_TAIL = sum(_TAIL_GROUPS)
HALO = CONV_W - 1
HALO_BASE = SUBLANES


def _rope(x, cos, sa, sb):
    rot = ROPE_DIM // 2
    return x * cos + pltpu.roll(x, rot, 1) * sa + pltpu.roll(x, LANES - rot, 1) * sb


def _proj_in_body(x_ref, nw_ref, w_ref, wt_ref, cbuf_ref, cw_ref, cos_ref, sa_ref, sb_ref, alog_ref, dtb_ref,
                  act_ref, z_ref, bg_ref, ag_ref, qb_ref, kb_ref, vb_ref, nbuf_ref, xp_ref, *, tm):
    j = pl.program_id(1)

    @pl.when(j == 0)
    def _():
        xp_ref[HALO_BASE - HALO:HALO_BASE, :] = cbuf_ref[...]

    h = _rms(x_ref[...], nw_ref[...]).astype(BF16)
    qkv = jnp.dot(h, w_ref[:, 0:A_QKV], preferred_element_type=F32)
    xp_ref[HALO_BASE:HALO_BASE + tm, :] = qkv
    z_ref[...] = jnp.dot(h, w_ref[:, A_QKV:O_TAIL], preferred_element_type=F32)
    offs = np.cumsum((0,) + _TAIL_GROUPS).tolist()
    bg, ag, qb, kb, vb = [jnp.dot(h, wt_ref[:, offs[i]:offs[i + 1]], preferred_element_type=F32)
                          for i in range(len(_TAIL_GROUPS))]
    bg_ref[...] = jax.nn.sigmoid(bg)
    a_in = ag + dtb_ref[...]
    softplus = jnp.maximum(a_in, 0.0) + jnp.log1p(jnp.exp(-jnp.abs(a_in)))
    ag_ref[...] = -jnp.exp(alog_ref[...]) * softplus
    vb_ref[...] = vb
    cos, sa, sb = cos_ref[...], sa_ref[...], sb_ref[...]
    kb_ref[...] = _rope(kb, cos, sa, sb)
    for s in range(B_QW // LANES):
        qb_ref[:, s * LANES:(s + 1) * LANES] = (
            _rope(qb[:, s * LANES:(s + 1) * LANES], cos, sa, sb) * (B_HD ** -0.5)).astype(BF16)

    cw = cw_ref[...]
    conv = qkv * cw[0:1, :]
    for t in range(1, CONV_W):
        conv = pltpu.roll(conv, 1, 0) + qkv * cw[t:t + 1, :]
    act_ref[...] = _silu(conv)
    lo = HALO_BASE - HALO
    head = xp_ref[lo:lo + SUBLANES, :] * cw[0:1, :]
    for t in range(1, CONV_W):
        head = head + xp_ref[lo + t:lo + t + SUBLANES, :] * cw[t:t + 1, :]
    act_ref[0:SUBLANES, :] = _silu(head)
    for hd in range(2 * A_HEADS):
        cols = slice(hd * A_DK, (hd + 1) * A_DK)
        t = act_ref[:, cols]
        t = t * lax.rsqrt(jnp.sum(t * t, axis=-1, keepdims=True) + EPS)
        act_ref[:, cols] = t * (A_DK ** -0.5) if hd < A_HEADS else t

    @pl.when(j == pl.num_programs(1) - 1)
    def _():
        nbuf_ref[...] = xp_ref[HALO_BASE + tm - HALO:HALO_BASE + tm, :]

    xp_ref[HALO_BASE - HALO:HALO_BASE, :] = xp_ref[HALO_BASE + tm - HALO:HALO_BASE + tm, :]


def _proj_in(x, nw, w, w_tail, cbuf, cw, cos, sa, sb, alog, dtb, *, layer, tm):
    b, l, _ = x.shape
    blk = lambda width: pl.BlockSpec((None, tm, width), lambda i, j: (i, j, 0))
    buf = pl.BlockSpec((None, HALO, A_QKV), lambda i, j: (i, 0, 0))
    tab = pl.BlockSpec((tm, LANES), lambda i, j: (j, 0))
    widths = [A_QKV, A_VW] + list(_TAIL_GROUPS)
    dtypes = [F32, F32, F32, F32, BF16, F32, F32]
    return pl.pallas_call(
        functools.partial(_proj_in_body, tm=tm),
        grid=(b, l // tm),
        in_specs=[blk(D_MODEL), _resident((1, D_MODEL), layer), _resident((D_MODEL, O_TAIL), layer),
                  _resident((D_MODEL, D_TAIL), layer), buf, _resident((CONV_W, A_QKV), layer), tab, tab, tab,
                  _resident((1, GATE_W), layer), _resident((1, GATE_W), layer)],
        out_specs=[blk(width) for width in widths] + [buf],
        out_shape=[jax.ShapeDtypeStruct((b, l, width), dt) for width, dt in zip(widths, dtypes)]
                  + [jax.ShapeDtypeStruct((b, HALO, A_QKV), F32)],
        scratch_shapes=[pltpu.VMEM((HALO_BASE + tm, A_QKV), F32)],
        compiler_params=_params(("parallel", "arbitrary")),
        name="proj_in",
    )(x, nw, w, w_tail, cbuf, cw, cos, sa, sb, alog, dtb)


def _chunk_cumsum(g, tril):
    g1 = g.astype(BF16)
    r1 = g - g1.astype(F32)
    g2 = r1.astype(BF16)
    g3 = (r1 - g2.astype(F32)).astype(BF16)
    parts = jnp.dot(tril, jnp.concatenate([g1, g2, g3], axis=1), preferred_element_type=F32)
    return parts[:, 0:LANES] + parts[:, LANES:2 * LANES] + parts[:, 2 * LANES:3 * LANES]


INV_BLOCK = 32


def _unit_lower_inverses_minus_eye(ls, c):
    bs = min(INV_BLOCK, c)
    row = lax.broadcasted_iota(jnp.int32, (c, c), 0)
    col = lax.broadcasted_iota(jnp.int32, (c, c), 1)
    same = lambda b: (row // b) == (col // b)
    diag = [jnp.where(same(bs), l, 0.0) for l in ls] if bs < c else ls
    p = [-d for d in diag]
    m = [_mm(d, d) for d in diag]
    for _ in range(int(np.log2(bs)) - 2):
        pm = [_mm(pi, mi) for pi, mi in zip(p, m)]
        p = [pi + mi + pmi for pi, mi, pmi in zip(p, m, pm)]
        m = [_mm(mi, mi) for mi in m]
    pm = [_mm(pi, mi) for pi, mi in zip(p, m)]
    p = [pi + mi + pmi for pi, mi, pmi in zip(p, m, pm)]
    b = bs
    while b < c:
        sel = same(2 * b) & jnp.logical_not(same(b))
        off = [jnp.where(sel, l, 0.0) for l in ls]
        x = [oi + _mm(pi, oi) for pi, oi in zip(p, off)]
        p = [pi - (xi + _mm(xi, pi)) for pi, xi in zip(p, x)]
        b *= 2
    return p


def _gdn_body(act_ref, z_ref, beta_ref, g_ref, s0_ref, gw_ref,
              o_ref, snew_ref, s_ref, u_ref, wq_ref, qk_ref, kdec_ref, *, tb, c, n_streams):
    t = pl.program_id(1)
    seq = tb // n_streams

    @pl.when(t == 0)
    def _():
        s_ref[...] = s0_ref[...]

    beta_all = beta_ref[...]
    g_all = g_ref[...]

    row = lax.broadcasted_iota(jnp.int32, (c, c), 0)
    col = lax.broadcasted_iota(jnp.int32, (c, c), 1)
    causal = row >= col
    strict = row > col
    tril = jnp.where(causal, 1.0, 0.0).astype(BF16)
    gw = gw_ref[...]
    n_chunks = tb // c

    g_tots = []
    lmats, rhss = [], []
    for ci in range(n_chunks):
        r0 = ci * c
        beta = beta_all[r0:r0 + c, :]
        g_cum = _chunk_cumsum(g_all[r0:r0 + c, :], tril)
        g_cum_t = g_cum.T
        g_last = g_cum[c - 1:c, :]
        e_g = jnp.exp(g_cum)
        e_gl = jnp.exp(g_last - g_cum)
        g_tots.append(jnp.exp(g_last))
        for h in range(A_HEADS):
            i = ci * A_HEADS + h
            q = act_ref[r0:r0 + c, h * A_DK:(h + 1) * A_DK]
            k = act_ref[r0:r0 + c, A_QK + h * A_DK:A_QK + (h + 1) * A_DK]
            v = act_ref[r0:r0 + c, 2 * A_QK + h * A_DV:2 * A_QK + (h + 1) * A_DV]
            b_col = beta[:, h:h + 1]
            eg_col = e_g[:, h:h + 1]
            decay = jnp.exp(jnp.where(causal, g_cum[:, h:h + 1] - g_cum_t[h:h + 1, :], -jnp.inf))
            k16 = k.astype(BF16)
            qk_kk = lax.dot_general(jnp.concatenate([q.astype(BF16), k16], axis=0), k16,
                                    (((1,), (1,)), ((), ())), preferred_element_type=F32)
            qk_ref[i] = (qk_kk[0:c, :] * decay).astype(BF16)
            lmats.append(jnp.where(strict, qk_kk[c:2 * c, :] * decay * b_col, 0.0))
            rhss.append(jnp.concatenate([v * b_col, k * (b_col * eg_col)], axis=1))
            wq_ref[i, c:2 * c, :] = (q * eg_col).astype(BF16)
            kdec_ref[i] = (k * e_gl[:, h:h + 1]).astype(BF16)
    tinvs = _unit_lower_inverses_minus_eye(lmats, c)
    for i, (tinv, rhs) in enumerate(zip(tinvs, rhss)):
        sol = rhs + _mm(tinv, rhs)
        u_ref[i] = sol[:, 0:A_DV]
        wq_ref[i, 0:c, :] = sol[:, A_DV:A_DV + A_DK].astype(BF16)

    for step in range(seq // c):
        lanes = [(sq, h, sq * (seq // c) + step) for sq in range(n_streams) for h in range(A_HEADS)]
        idx = [ci * A_HEADS + h for sq, h, ci in lanes]
        s_old = [s_ref[sq, h] for sq, h, ci in lanes]
        wq_s = [jnp.dot(wq_ref[i], st.astype(BF16), preferred_element_type=F32) for i, st in zip(idx, s_old)]
        v_new = [u_ref[i] - x[0:c, :] for i, x in zip(idx, wq_s)]
        v16 = [x.astype(BF16) for x in v_new]
        for k, (sq, h, ci) in enumerate(lanes):
            s_ref[sq, h] = s_old[k] * g_tots[ci][:, h:h + 1] + lax.dot_general(
                kdec_ref[idx[k]], v16[k], (((0,), (0,)), ((), ())), preferred_element_type=F32)
        for k, (sq, h, ci) in enumerate(lanes):
            r0 = ci * c
            o = wq_s[k][c:2 * c, :] + jnp.dot(qk_ref[idx[k]], v16[k], preferred_element_type=F32)
            o = o * lax.rsqrt(jnp.mean(o * o, axis=-1, keepdims=True) + EPS) * gw
            o_ref[r0:r0 + c, h * A_DV:(h + 1) * A_DV] = o * _silu(z_ref[r0:r0 + c, h * A_DV:(h + 1) * A_DV])

    @pl.when(t == pl.num_programs(1) - 1)
    def _():
        snew_ref[...] = s_ref[...]


def _gdn(act, z, beta, g, s0, gw, *, tb, c, n_streams=1):
    b, l, _ = act.shape
    nt = l // tb
    assert n_streams == 1 or nt == 1
    n_items = (tb // c) * A_HEADS
    blk = lambda width: pl.BlockSpec((None, tb, width), lambda i, j: (i, j, 0))
    state = pl.BlockSpec((n_streams, A_HEADS, A_DK, A_DV), lambda i, j: (i, 0, 0, 0))
    lane_row = pl.BlockSpec((1, LANES), lambda i, j: (0, 0))
    return pl.pallas_call(
        functools.partial(_gdn_body, tb=tb, c=c, n_streams=n_streams),
        grid=(b, nt),
        in_specs=[blk(A_QKV), blk(A_VW), blk(GATE_W), blk(GATE_W), state, lane_row],
        out_specs=[blk(A_VW), state],
        out_shape=[jax.ShapeDtypeStruct((b, l, A_VW), F32),
                   jax.ShapeDtypeStruct((b * n_streams, A_HEADS, A_DK, A_DV), F32)],
        scratch_shapes=[pltpu.VMEM((n_streams, A_HEADS, A_DK, A_DV), F32),
                        pltpu.VMEM((n_items, c, A_DV), F32),
                        pltpu.VMEM((n_items, 2 * c, A_DK), BF16),
                        pltpu.VMEM((n_items, c, c), BF16),
                        pltpu.VMEM((n_items, c, A_DK), BF16)],
        compiler_params=_params(("parallel", "arbitrary")),
        name="gdn",
    )(act, z, beta, g, s0, gw)


def _rope_table_body(inv_ref, ma_ref, mb_ref, cos_ref, sa_ref, sb_ref, c0_ref, s0_ref, *, tm, pos0):
    i = pl.program_id(0)
    inv = inv_ref[...]

    @pl.when(i == 0)
    def _():
        ang = lax.broadcasted_iota(jnp.int32, (tm, LANES), 0).astype(F32) * inv
        c0_ref[...] = jnp.cos(ang)
        s0_ref[...] = jnp.sin(ang)

    base = (pos0 + i * tm).astype(F32) * inv
    cb = jnp.cos(base)
    sn = jnp.sin(base)
    c0 = c0_ref[...]
    s0 = s0_ref[...]
    sin = s0 * cb + c0 * sn
    cos_ref[...] = c0 * cb - s0 * sn
    sa_ref[...] = sin * ma_ref[...]
    sb_ref[...] = -sin * mb_ref[...]


def _rope_tables(inv_row, ma, mb, *, length, pos0, tm):
    row = pl.BlockSpec((1, LANES), lambda i: (0, 0))
    out = pl.BlockSpec((tm, LANES), lambda i: (i, 0))
    return pl.pallas_call(
        functools.partial(_rope_table_body, tm=tm, pos0=pos0),
        grid=(length // tm,),
        in_specs=[row, row, row],
        out_specs=[out, out, out],
        out_shape=[jax.ShapeDtypeStruct((length, LANES), F32)] * 3,
        scratch_shapes=[pltpu.VMEM((tm, LANES), F32), pltpu.VMEM((tm, LANES), F32)],
        compiler_params=_params(("arbitrary",)),
        name="rope_tables",
    )(inv_row, ma, mb)


SWA_LOCKSTEP = 4


def _swa_body(qs_ref, kb_ref, vb_ref, kc_ref, vc_ref, sinks_ref,
              o_ref, newk_ref, newv_ref, kx_ref, vx_ref, kvar_ref, vvar_ref, *, tb, c, mask_start):
    t = pl.program_id(1)
    half = B_HD
    w = WINDOW + c

    @pl.when(t == 0)
    def _():
        kx_ref[0:WINDOW, :] = kc_ref[...]
        vx_ref[0:WINDOW, :] = vc_ref[...]

    kx_ref[WINDOW:WINDOW + tb, :] = kb_ref[...]
    vx_ref[WINDOW:WINDOW + tb, :] = vb_ref[...]

    lo = lax.broadcasted_iota(jnp.int32, (WINDOW + tb, LANES), 1) < half
    for src, dst in ((kx_ref, kvar_ref), (vx_ref, vvar_ref)):
        full = src[...]
        swapped = pltpu.roll(full, half, 1)
        dst[0] = jnp.where(lo, full, 0.0).astype(BF16)
        dst[1] = jnp.where(lo, 0.0, swapped).astype(BF16)
        dst[2] = jnp.where(lo, swapped, 0.0).astype(BF16)
        dst[3] = jnp.where(lo, 0.0, full).astype(BF16)

    first_rows = lax.broadcasted_iota(jnp.int32, (2 * c, 1), 0) < c
    kcol = lax.broadcasted_iota(jnp.int32, (2 * c, w), 1)

    sks = [jnp.where(first_rows, sinks_ref[4 * g + hh], sinks_ref[4 * g + 2 + hh])
           for g in range(B_KV_HEADS) for hh in range(2)]
    n_chunks = tb // c
    for i0 in range(0, n_chunks, SWA_LOCKSTEP):
        items = [(i, g, hh) for i in range(i0, min(i0 + SWA_LOCKSTEP, n_chunks))
                 for g in range(B_KV_HEADS) for hh in range(2)]
        lhs = {(i, g): jnp.concatenate([qs_ref[i * c:(i + 1) * c, (2 * g) * LANES:(2 * g + 1) * LANES],
                                        qs_ref[i * c:(i + 1) * c, (2 * g + 1) * LANES:(2 * g + 2) * LANES]], axis=0)
               for i, g, hh in items if hh == 0}
        s = [lax.dot_general(lhs[i, g], kvar_ref[2 * g + hh, i * c:i * c + w, :],
                             (((1,), (1,)), ((), ())), preferred_element_type=F32) for i, g, hh in items]
        if mask_start:
            s = [jnp.where(t * tb + i * c - WINDOW + kcol >= 0, si, -jnp.inf) for si, (i, g, hh) in zip(s, items)]
        m = [jnp.maximum(jnp.max(si, axis=-1, keepdims=True), sks[2 * g + hh]) for si, (i, g, hh) in zip(s, items)]
        p = [jnp.exp(si - mi) for si, mi in zip(s, m)]
        den = [jnp.sum(pi, axis=-1, keepdims=True) + jnp.exp(sks[2 * g + hh] - mi)
               for pi, mi, (i, g, hh) in zip(p, m, items)]
        pv = [jnp.dot(pi.astype(BF16), vvar_ref[2 * g + hh, i * c:i * c + w, :], preferred_element_type=F32) / di
              for pi, di, (i, g, hh) in zip(p, den, items)]
        for j in range(0, len(items), 2):
            i, g, _ = items[j]
            o = pv[j] + pv[j + 1]
            o_ref[i * c:(i + 1) * c, (2 * g) * LANES:(2 * g + 1) * LANES] = o[0:c, :]
            o_ref[i * c:(i + 1) * c, (2 * g + 1) * LANES:(2 * g + 2) * LANES] = o[c:2 * c, :]

    @pl.when(t == pl.num_programs(1) - 1)
    def _():
        newk_ref[...] = kx_ref[tb:tb + WINDOW, :]
        newv_ref[...] = vx_ref[tb:tb + WINDOW, :]

    if tb >= WINDOW:
        kx_ref[0:WINDOW, :] = kx_ref[tb:tb + WINDOW, :]
        vx_ref[0:WINDOW, :] = vx_ref[tb:tb + WINDOW, :]


def _swa(qb, kb, vb, kc, vc, sinks, *, tb, c, mask_start):
    b, l, _ = qb.shape
    nt = l // tb
    assert tb >= WINDOW or nt == 1
    blk = lambda width: pl.BlockSpec((None, tb, width), lambda i, j: (i, j, 0))
    cache = pl.BlockSpec((None, WINDOW, B_KW), lambda i, j: (i, 0, 0))
    return pl.pallas_call(
        functools.partial(_swa_body, tb=tb, c=c, mask_start=mask_start),
        grid=(b, nt),
        in_specs=[blk(B_QW), blk(B_KW), blk(B_KW), cache, cache, pl.BlockSpec(memory_space=pltpu.SMEM)],
        out_specs=[blk(B_QW), cache, cache],
        out_shape=[jax.ShapeDtypeStruct((b, l, B_QW), F32),
                   jax.ShapeDtypeStruct((b, WINDOW, B_KW), F32),
                   jax.ShapeDtypeStruct((b, WINDOW, B_KW), F32)],
        scratch_shapes=[pltpu.VMEM((WINDOW + tb, B_KW), F32),
                        pltpu.VMEM((WINDOW + tb, B_KW), F32),
                        pltpu.VMEM((4, WINDOW + tb, LANES), BF16),
                        pltpu.VMEM((4, WINDOW + tb, LANES), BF16)],
        compiler_params=_params(("parallel", "arbitrary")),
        name="swa",
    )(qb, kb, vb, kc, vc, sinks)


def _pack_w_in_tail(w):
    offs = np.cumsum([O_TAIL, A_HEADS, A_HEADS, B_QW, B_KW, B_KW]).tolist()
    bg, ag, qb, kb, vb = [w[..., offs[i]:offs[i + 1]] for i in range(5)]
    pad = lambda g: jnp.pad(g, ((0, 0), (0, 0), (0, GATE_W - g.shape[-1])))
    return jnp.concatenate([pad(bg), pad(ag), qb, kb, vb], axis=-1).astype(BF16)


GDN_CHUNK = 128
SWA_CHUNK = 64


def _tiles(batch, length):
    if length >= 1024:
        return dict(tm=1024, proj_tm=1024, gdn_tb=1024, gdn_c=GDN_CHUNK, swa_tb=512, swa_c=SWA_CHUNK)
    return dict(tm=batch * length, proj_tm=length, gdn_tb=GDN_CHUNK, gdn_c=GDN_CHUNK, swa_tb=length, swa_c=length)


def _pad_lanes(v):
    return jnp.pad(v, ((0, 0), (0, GATE_W - v.shape[-1])))[:, None, :]


def _trunk(x, rope_tabs, conv_bufs, s0s, k_caches, v_caches, wts, *, batch, length, tm, proj_tm,
           gdn_tb, gdn_c, swa_tb, swa_c, mask_start):
    (norm_ff1, ff1_in, ff1_out, norm_mix, w_in, w_in_tail, conv_w, alog, dtb, gnorm, sinks, w_out,
     norm_ff2, ff2_in, ff2_out, norm_final) = wts
    cos, sa, sb = rope_tabs
    lpad = -(-length // gdn_c) * gdn_c
    bufs, states, ks, vs = [], [], [], []

    def pad_rows(a):
        return a if lpad == length else jnp.pad(a, ((0, 0), (0, lpad - length), (0, 0)))

    for l in range(DEPTH):
        x = _ffn(x, norm_ff1, ff1_in, ff1_out, layer=l, tm=tm)
        act, z, bg, ag, qb, kb, vb, nbuf = _proj_in(x.reshape(batch, length, D_MODEL), norm_mix, w_in, w_in_tail,
                                                    conv_bufs[l], conv_w, cos, sa, sb, alog, dtb, layer=l, tm=proj_tm)
        if length < gdn_c:
            one = lambda a: pad_rows(a).reshape(1, batch * lpad, a.shape[-1])
            o_a, s_new = _gdn(one(act), one(z), one(bg), one(ag), s0s[l], gnorm[l],
                              tb=batch * lpad, c=gdn_c, n_streams=batch)
            o_a = o_a.reshape(batch, lpad, A_VW)
        else:
            o_a, s_new = _gdn(act, z, bg, ag, s0s[l], gnorm[l], tb=gdn_tb, c=gdn_c)
        o_b, nk, nv = _swa(qb, kb, vb, k_caches[l], v_caches[l], sinks[l],
                           tb=swa_tb, c=swa_c, mask_start=mask_start)
        o_a = o_a[:, :length].reshape(batch * length, A_VW)
        x = _mix_out_ffn(x, o_a, o_b.reshape(batch * length, B_QW), w_out, norm_ff2, ff2_in, ff2_out,
                         norm_final if l == DEPTH - 1 else None, layer=l, tm=tm)
        bufs.append(nbuf)
        states.append(s_new)
        ks.append(nk.reshape(batch, WINDOW, B_KV_HEADS, B_HD))
        vs.append(nv.reshape(batch, WINDOW, B_KV_HEADS, B_HD))
    y = x.reshape(batch, length, D_MODEL)
    return y, jnp.stack(bufs), jnp.stack(states), jnp.stack(ks), jnp.stack(vs)


def kernel(x_prompt, x_sample, cache_conv, state_delta, cache_k, cache_v, norm_ff1, ff1_w_in, ff1_w_out, norm_mix, w_mix_in, conv_w, a_log, dt_bias, gnorm_w, sinks, w_mix_out, norm_ff2, ff2_w_in, ff2_w_out, norm_final):
    bp, lp, _ = x_prompt.shape
    bs, ls, _ = x_sample.shape
    rows = cache_k.shape[2]
    assert rows == WINDOW

    wts = (norm_ff1[:, None, :], ff1_w_in.astype(BF16), ff1_w_out.astype(BF16), norm_mix[:, None, :],
           w_mix_in[..., :O_TAIL].astype(BF16), _pack_w_in_tail(w_mix_in), conv_w, _pad_lanes(a_log), _pad_lanes(dt_bias), gnorm_w[:, None, :], sinks,
           w_mix_out.astype(BF16), norm_ff2[:, None, :], ff2_w_in.astype(BF16), ff2_w_out.astype(BF16),
           norm_final[None, :])

    inv = jnp.power(ROPE_THETA, -jnp.arange(0, ROPE_DIM, 2, dtype=F32) / ROPE_DIM)
    rot = ROPE_DIM // 2
    head_row = jnp.concatenate([inv, inv, jnp.zeros((B_HD - ROPE_DIM,), F32)])
    inv_row = jnp.tile(head_row, LANES // B_HD)[None, :]
    d = np.arange(LANES) % B_HD
    ma = jnp.asarray(((d >= rot) & (d < ROPE_DIM)).astype(np.float32))[None, :]
    mb = jnp.asarray((d < rot).astype(np.float32))[None, :]

    zero_buf = jnp.zeros((DEPTH, bp, CONV_W - 1, A_QKV), F32)
    zero_state = jnp.zeros((DEPTH, bp, A_HEADS, A_DK, A_DV), F32)
    zero_kv = jnp.zeros((DEPTH, bp, WINDOW, B_KW), F32)
    tabs_p = _rope_tables(inv_row, ma, mb, length=lp, pos0=0, tm=512)
    y_p, conv_p, delta_p, k_p, v_p = _trunk(
        x_prompt.reshape(bp * lp, D_MODEL), tabs_p, zero_buf, zero_state, zero_kv, zero_kv, wts,
        batch=bp, length=lp, mask_start=True, **_tiles(bp, lp))

    tabs_s = _rope_tables(inv_row, ma, mb, length=ls, pos0=PAST_LEN, tm=ls)
    y_s, conv_s, delta_s, k_s, v_s = _trunk(
        x_sample.reshape(bs * ls, D_MODEL), tabs_s, cache_conv, state_delta,
        cache_k.reshape(DEPTH, bs, WINDOW, B_KW), cache_v.reshape(DEPTH, bs, WINDOW, B_KW), wts,
        batch=bs, length=ls, mask_start=False, **_tiles(bs, ls))
    return (y_p, y_s, conv_p, delta_p, k_p, v_p, conv_s, delta_s, k_s, v_s)
```
